```python
import math
import jax, jax.numpy as jnp
from jax import lax
import numpy as np

D_MODEL = 4096
BATCH = 1
SEQ = 8192
DEPTH = 2

N_A_LAYERS = DEPTH // 2
N_B_LAYERS = DEPTH - N_A_LAYERS

D_FF = 11008
MACARON_WEIGHT = 0.5

SGU_WIDTH = D_MODEL
SGU_CHUNK = 128
SGU_GROUPS = 32
SGU_GROUP_DIM = SGU_WIDTH // SGU_GROUPS

N_HEADS = 32
HEAD_DIM = D_MODEL // N_HEADS
MOBA_BLOCK = 256
MOBA_TOPK = 3
QUERY_CHUNK = 32
ROPE_THETA = 10000.0

NORM_EPS = 1e-6
NEG_BIG = -1e30

kernel_name = "yoco_gmlp_moba_macaron_trunk"


def rmsnorm(x, g):
    x32 = x.astype(jnp.float32)
    y = x32 * lax.rsqrt(jnp.mean(x32 * x32, axis=-1, keepdims=True) + NORM_EPS)
    return (y * g.astype(jnp.float32)).astype(x.dtype)


def swiglu_ffn(h, w13, w2):
    a = h @ w13
    gate, up = jnp.split(a, 2, axis=-1)
    return (jax.nn.silu(gate) * up) @ w2


def rope(t, positions):
    dh = t.shape[-1]
    inv_freq = 1.0 / (ROPE_THETA ** (jnp.arange(0, dh, 2, dtype=jnp.float32) / dh))
    ang = positions.astype(jnp.float32)[:, None] * inv_freq[None, :]
    ang = jnp.concatenate([ang, ang], axis=-1)
    cos = jnp.cos(ang).astype(t.dtype)
    sin = jnp.sin(ang).astype(t.dtype)
    t1, t2 = jnp.split(t, 2, axis=-1)
    rot = jnp.concatenate([-t2, t1], axis=-1)
    return t * cos + rot * sin


def chunked_sgu_mixer(h, w_in, sgu_norm, w_spatial, b_spatial, w_out):
    B, S, _ = h.shape
    z = jax.nn.gelu(h @ w_in, approximate=False)
    u, v = jnp.split(z, 2, axis=-1)
    v = rmsnorm(v, sgu_norm)
    causal = jnp.tril(jnp.ones((SGU_CHUNK, SGU_CHUNK), dtype=bool))
    ws = jnp.where(causal[None], w_spatial, jnp.zeros_like(w_spatial))
    vc = v.reshape(B, S // SGU_CHUNK, SGU_CHUNK, SGU_GROUPS, SGU_GROUP_DIM)
    mixed = jnp.einsum('gts,bcsgd->bctgd', ws, vc) + b_spatial.T[None, None, :, :, None]
    out = u * mixed.reshape(B, S, SGU_WIDTH)
    return out @ w_out


def shared_kv(x, kv_norm, w_kv):
    B, S, _ = x.shape
    h = rmsnorm(x, kv_norm)
    kv = (h @ w_kv).reshape(B, S, 2, N_HEADS, HEAD_DIM)
    k = kv[:, :, 0].transpose(0, 2, 1, 3)
    v = kv[:, :, 1].transpose(0, 2, 1, 3)
    k = rope(k, jnp.arange(S))
    n_blocks = -(-S // MOBA_BLOCK)
    pad = n_blocks * MOBA_BLOCK - S
    k = jnp.pad(k, ((0, 0), (0, 0), (0, pad), (0, 0)))
    v = jnp.pad(v, ((0, 0), (0, 0), (0, pad), (0, 0)))
    kmean = jnp.mean(k.reshape(B, N_HEADS, n_blocks, MOBA_BLOCK, HEAD_DIM).astype(jnp.float32),
                     axis=3).astype(k.dtype)
    return k, v, kmean


def moba_attention(q, k, v, kmean):
    B, H, S, Dh = q.shape
    n_blocks = kmean.shape[2]
    topk = min(MOBA_TOPK, n_blocks)
    scale = 1.0 / math.sqrt(Dh)
    pos = jnp.arange(S)
    qblk = pos // MOBA_BLOCK
    gate = jnp.einsum('bhsd,bhnd->bhsn', q, kmean).astype(jnp.float32)
    past = jnp.arange(n_blocks)[None, :] < qblk[:, None]
    gate = jnp.where(past[None, None], gate, NEG_BIG)
    _, sel = lax.top_k(gate, topk)
    sel_valid = jnp.arange(topk)[None, :] < qblk[:, None]

    kb = k.reshape(B, H, n_blocks, MOBA_BLOCK, Dh)
    vb = v.reshape(B, H, n_blocks, MOBA_BLOCK, Dh)
    n_q = S // QUERY_CHUNK
    qc = q.reshape(B, H, n_q, QUERY_CHUNK, Dh).transpose(2, 0, 1, 3, 4)
    selc = sel.reshape(B, H, n_q, QUERY_CHUNK, topk).transpose(2, 0, 1, 3, 4)
    validc = sel_valid.reshape(n_q, QUERY_CHUNK, topk)
    bi = jnp.arange(B)[:, None, None, None]
    hi = jnp.arange(H)[None, :, None, None]

    def step(args):
        c, qi, si, vi = args
        k_sel = kb[bi, hi, si]
        v_sel = vb[bi, hi, si]
        s_sel = jnp.einsum('bhqd,bhqtkd->bhqtk', qi, k_sel).astype(jnp.float32) * scale
        s_sel = jnp.where(vi[None, None, :, :, None], s_sel, NEG_BIG)
        q0 = c * QUERY_CHUNK
        start = (q0 // MOBA_BLOCK) * MOBA_BLOCK
        k_own = lax.dynamic_slice_in_dim(k, start, MOBA_BLOCK, axis=2)
        v_own = lax.dynamic_slice_in_dim(v, start, MOBA_BLOCK, axis=2)
        s_own = jnp.einsum('bhqd,bhkd->bhqk', qi, k_own).astype(jnp.float32) * scale
        qpos = q0 + jnp.arange(QUERY_CHUNK)
        kpos = start + jnp.arange(MOBA_BLOCK)
        s_own = jnp.where((kpos[None, :] <= qpos[:, None])[None, None], s_own, NEG_BIG)
        logits = jnp.concatenate([s_sel.reshape(B, H, QUERY_CHUNK, topk * MOBA_BLOCK), s_own], axis=-1)
        p = jax.nn.softmax(logits, axis=-1).astype(v.dtype)
        p_sel = p[..., :topk * MOBA_BLOCK].reshape(B, H, QUERY_CHUNK, topk, MOBA_BLOCK)
        p_own = p[..., topk * MOBA_BLOCK:]
        return (jnp.einsum('bhqtk,bhqtkd->bhqd', p_sel, v_sel)
                + jnp.einsum('bhqk,bhkd->bhqd', p_own, v_own))

    out = lax.map(step, (jnp.arange(n_q), qc, selc, validc))
    return out.transpose(1, 2, 0, 3, 4).reshape(B, H, S, Dh)


def moba_mixer(h, k, v, kmean, w_q, w_o):
    B, S, _ = h.shape
    q = (h @ w_q).reshape(B, S, N_HEADS, HEAD_DIM).transpose(0, 2, 1, 3)
    q = rope(q, jnp.arange(S))
    o = moba_attention(q, k, v, kmean)
    return o.transpose(0, 2, 1, 3).reshape(B, S, N_HEADS * HEAD_DIM) @ w_o


def setup_inputs(seed: int = 0) -> dict:
    key = jax.random.key(seed)
    ks = jax.random.split(key, 16)
    D, F, E, G, C = D_MODEL, D_FF, SGU_WIDTH, SGU_GROUPS, SGU_CHUNK
    HD = N_HEADS * HEAD_DIM
    nrm = lambda k, shape, fan_in: jax.random.normal(k, shape, jnp.float32) * (fan_in ** -0.5)
    gain = lambda k, shape: 1.0 + 0.02 * jax.random.normal(k, shape, jnp.float32)
    return {
        "x": jax.random.normal(ks[0], (BATCH, SEQ, D), jnp.float32),
        "ffn_norm": gain(ks[1], (DEPTH, 2, D)),
        "ffn_w13": nrm(ks[2], (DEPTH, 2, D, 2 * F), D),
        "ffn_w2": nrm(ks[3], (DEPTH, 2, F, D), F),
        "mix_norm": gain(ks[4], (DEPTH, D)),
        "a_w_in": nrm(ks[5], (N_A_LAYERS, D, 2 * E), D),
        "a_sgu_norm": gain(ks[6], (N_A_LAYERS, E)),
        "a_w_spatial": nrm(ks[7], (N_A_LAYERS, G, C, C), C),
        "a_b_spatial": 1.0 + 0.1 * jax.random.normal(ks[8], (N_A_LAYERS, G, C), jnp.float32),
        "a_w_out": nrm(ks[9], (N_A_LAYERS, E, D), E),
        "kv_norm": gain(ks[10], (D,)),
        "w_kv": nrm(ks[11], (D, 2 * HD), D),
        "b_w_q": nrm(ks[12], (N_B_LAYERS, D, HD), D),
        "b_w_o": nrm(ks[13], (N_B_LAYERS, HD, D), HD),
        "final_norm": gain(ks[14], (D,)),
    }


def reference(x, ffn_norm, ffn_w13, ffn_w2, mix_norm, a_w_in, a_sgu_norm, a_w_spatial,
              a_b_spatial, a_w_out, kv_norm, w_kv, b_w_q, b_w_o, final_norm):
    k = v = kmean = None
    for layer in range(DEPTH):
        if layer == N_A_LAYERS:
            k, v, kmean = shared_kv(x, kv_norm, w_kv)
        x = x + MACARON_WEIGHT * swiglu_ffn(rmsnorm(x, ffn_norm[layer, 0]),
                                            ffn_w13[layer, 0], ffn_w2[layer, 0])
        h = rmsnorm(x, mix_norm[layer])
        if layer < N_A_LAYERS:
            a = layer
            x = x + chunked_sgu_mixer(h, a_w_in[a], a_sgu_norm[a], a_w_spatial[a],
                                      a_b_spatial[a], a_w_out[a])
        else:
            b = layer - N_A_LAYERS
            x = x + moba_mixer(h, k, v, kmean, b_w_q[b], b_w_o[b])
        x = x + MACARON_WEIGHT * swiglu_ffn(rmsnorm(x, ffn_norm[layer, 1]),
                                            ffn_w13[layer, 1], ffn_w2[layer, 1])
    return rmsnorm(x, final_norm)
```

```python
import functools
import math

import jax
import jax.numpy as jnp
from jax import lax
from jax.experimental import pallas as pl
from jax.experimental.pallas import tpu as pltpu

F32 = jnp.float32
BF16 = jnp.bfloat16

NORM_EPS = 1e-6
NEG_BIG = -1e30
MACARON_WEIGHT = 0.5
ROPE_THETA = 10000.0

SGU_CHUNK = 128
SGU_GROUPS = 32
N_HEADS = 32
HEAD_DIM = 128
MOBA_BLOCK = 256
MOBA_TOPK = 3

LANES = 128
VMEM_LIMIT_BYTES = 56 * 1024 * 1024

NORM_ROWS = 128


def _params(semantics):
    return pltpu.CompilerParams(dimension_semantics=semantics, vmem_limit_bytes=VMEM_LIMIT_BYTES)


def _fill_normed(x_ref, g_ref, hn_ref):
    gain = g_ref[...]

    def body(r, carry):
        rows = pl.ds(pl.multiple_of(r * NORM_ROWS, NORM_ROWS), NORM_ROWS)
        xr = x_ref[rows, :]
        inv = lax.rsqrt(jnp.mean(xr * xr, axis=-1, keepdims=True) + NORM_EPS)
        hn_ref[rows, :] = (x_ref[rows, :] * inv * gain).astype(hn_ref.dtype)
        return carry

    lax.fori_loop(0, x_ref.shape[0] // NORM_ROWS, body, 0)


def _swiglu_up_kernel(x_ref, g_ref, wg_ref, wu_ref, o_ref, hn_ref):
    @pl.when(pl.program_id(1) == 0)
    def _():
        _fill_normed(x_ref, g_ref, hn_ref)

    gate = jnp.dot(hn_ref[...], wg_ref[...], preferred_element_type=F32)
    up = jnp.dot(hn_ref[...], wu_ref[...], preferred_element_type=F32)
    o_ref[...] = (jax.nn.silu(gate) * up).astype(o_ref.dtype)


def _swiglu_up(x, gain, w13, layer, half, *, tm=1024, tn=256):
    s, d = x.shape
    f = w13.shape[-1] // 2
    nj = f // tn
    return pl.pallas_call(
        _swiglu_up_kernel,
        grid=(s // tm, nj),
        in_specs=[
            pl.BlockSpec((tm, d), lambda i, j: (i, 0)),
            pl.BlockSpec((1, d), lambda i, j: (0, 0)),
            pl.BlockSpec((None, None, d, tn), lambda i, j: (layer, half, 0, j)),
            pl.BlockSpec((None, None, d, tn), lambda i, j: (layer, half, 0, j + nj)),
        ],
        out_specs=pl.BlockSpec((tm, tn), lambda i, j: (i, j)),
        out_shape=jax.ShapeDtypeStruct((s, f), BF16),
        scratch_shapes=[pltpu.VMEM((tm, d), BF16)],
        compiler_params=_params(("parallel", "arbitrary")),
        name="swiglu_up",
    )(x, gain, w13, w13)


def _gelu_in_kernel(x_ref, g_ref, w_ref, o_ref, hn_ref):
    @pl.when(pl.program_id(1) == 0)
    def _():
        _fill_normed(x_ref, g_ref, hn_ref)

    z = jnp.dot(hn_ref[...], w_ref[...], preferred_element_type=F32)
    o_ref[...] = (0.5 * z * (1.0 + lax.erf(z * math.sqrt(0.5)))).astype(o_ref.dtype)


def _gelu_in(x, gain, w, *, tm=1024, tn=512):
    s, d = x.shape
    n = w.shape[-1]
    return pl.pallas_call(
        _gelu_in_kernel,
        grid=(s // tm, n // tn),
        in_specs=[
            pl.BlockSpec((tm, d), lambda i, j: (i, 0)),
            pl.BlockSpec((1, d), lambda i, j: (0, 0)),
            pl.BlockSpec((d, tn), lambda i, j: (0, j)),
        ],
        out_specs=pl.BlockSpec((tm, tn), lambda i, j: (i, j)),
        out_shape=jax.ShapeDtypeStruct((s, n), BF16),
        scratch_shapes=[pltpu.VMEM((tm, d), BF16)],
        compiler_params=_params(("parallel", "arbitrary")),
        name="gelu_in",
    )(x, gain, w)


def _rope_store(acc, cos, sin_signed, o_ref, scale):
    for h in range(acc.shape[1] // HEAD_DIM):
        cols = slice(h * HEAD_DIM, (h + 1) * HEAD_DIM)
        t = acc[:, cols]
        out = t * cos + pltpu.roll(t, HEAD_DIM // 2, axis=1) * sin_signed
        if scale != 1.0:
            out = out * scale
        o_ref[:, cols] = out.astype(o_ref.dtype)


def _rope_proj_kernel(x_ref, g_ref, w_ref, cos_ref, sin_ref, o_ref, hn_ref, *, n_rope_tiles, scale):
    j = pl.program_id(1)

    @pl.when(j == 0)
    def _():
        _fill_normed(x_ref, g_ref, hn_ref)

    @pl.when(j < n_rope_tiles)
    def _():
        acc = jnp.dot(hn_ref[...], w_ref[...], preferred_element_type=F32)
        _rope_store(acc, cos_ref[...], sin_ref[...], o_ref, scale)

    @pl.when(j >= n_rope_tiles)
    def _():
        o_ref[...] = jnp.dot(hn_ref[...], w_ref[...], preferred_element_type=F32).astype(o_ref.dtype)


def _rope_proj(x, gain, w, cos, sin_signed, *, rope_cols, scale, tm=1024, tn=512):
    s, d = x.shape
    n = w.shape[-1]
    kern = functools.partial(_rope_proj_kernel, n_rope_tiles=rope_cols // tn, scale=scale)
    return pl.pallas_call(
        kern,
        grid=(s // tm, n // tn),
        in_specs=[
            pl.BlockSpec((tm, d), lambda i, j: (i, 0)),
            pl.BlockSpec((1, d), lambda i, j: (0, 0)),
            pl.BlockSpec((d, tn), lambda i, j: (0, j)),
            pl.BlockSpec((tm, HEAD_DIM), lambda i, j: (i, 0)),
            pl.BlockSpec((tm, HEAD_DIM), lambda i, j: (i, 0)),
        ],
        out_specs=pl.BlockSpec((tm, tn), lambda i, j: (i, j)),
        out_shape=jax.ShapeDtypeStruct((s, n), BF16),
        scratch_shapes=[pltpu.VMEM((tm, d), BF16)],
        compiler_params=_params(("parallel", "arbitrary")),
        name="rope_proj",
    )(x, gain, w, cos, sin_signed)


def _matmul_residual_kernel(a_ref, w_ref, r_ref, o_ref, *, alpha):
    y = jnp.dot(a_ref[...], w_ref[...], preferred_element_type=F32)
    if alpha != 1.0:
        y = alpha * y
    o_ref[...] = r_ref[...] + y


def _matmul_residual(a, w, w_index, res, *, alpha, tm, tn):
    s, k = a.shape
    n = w.shape[-1]
    lead = (None,) * len(w_index)
    return pl.pallas_call(
        functools.partial(_matmul_residual_kernel, alpha=alpha),
        grid=(s // tm, n // tn),
        in_specs=[
            pl.BlockSpec((tm, k), lambda i, j: (i, 0)),
            pl.BlockSpec(lead + (k, tn), lambda i, j: w_index + (0, j)),
            pl.BlockSpec((tm, tn), lambda i, j: (i, j)),
        ],
        out_specs=pl.BlockSpec((tm, tn), lambda i, j: (i, j)),
        out_shape=jax.ShapeDtypeStruct((s, n), F32),
        compiler_params=_params(("parallel", "parallel")),
        name="matmul_residual",
    )(a, w, res)


def _sgu_kernel(u_ref, v_ref, gn_ref, ws_ref, b_ref, o_ref, wsm_ref, vn_ref):
    @pl.when(pl.program_id(0) == 0)
    def _():
        t = lax.broadcasted_iota(jnp.int32, (SGU_CHUNK, SGU_CHUNK), 0)
        s = lax.broadcasted_iota(jnp.int32, (SGU_CHUNK, SGU_CHUNK), 1)
        for g in range(SGU_GROUPS):
            wsm_ref[g] = jnp.where(s <= t, ws_ref[g], 0.0).astype(wsm_ref.dtype)

    v = v_ref[...].astype(F32)
    inv = lax.rsqrt(jnp.mean(v * v, axis=-1, keepdims=True) + NORM_EPS)
    vn_ref[...] = (v * inv * gn_ref[...]).astype(vn_ref.dtype)
    for g in range(SGU_GROUPS):
        cols = slice(g * LANES, (g + 1) * LANES)
        mixed = jnp.dot(wsm_ref[g], vn_ref[:, cols], preferred_element_type=F32) + b_ref[g]
        o_ref[:, cols] = (u_ref[:, cols].astype(F32) * mixed).astype(o_ref.dtype)


def _sgu_mix(z, sgu_gain, w_spatial, b_spatial):
    s, two_e = z.shape
    e = two_e // 2
    return pl.pallas_call(
        _sgu_kernel,
        grid=(s // SGU_CHUNK,),
        in_specs=[
            pl.BlockSpec((SGU_CHUNK, e), lambda c: (c, 0)),
            pl.BlockSpec((SGU_CHUNK, e), lambda c: (c, 1)),
            pl.BlockSpec((1, e), lambda c: (0, 0)),
            pl.BlockSpec((SGU_GROUPS, SGU_CHUNK, SGU_CHUNK), lambda c: (0, 0, 0)),
            pl.BlockSpec((SGU_GROUPS, SGU_CHUNK, 1), lambda c: (0, 0, 0)),
        ],
        out_specs=pl.BlockSpec((SGU_CHUNK, e), lambda c: (c, 0)),
        out_shape=jax.ShapeDtypeStruct((s, e), BF16),
        scratch_shapes=[
            pltpu.VMEM((SGU_GROUPS, SGU_CHUNK, SGU_CHUNK), BF16),
            pltpu.VMEM((SGU_CHUNK, e), BF16),
        ],
        compiler_params=_params(("arbitrary",)),
        name="sgu_mix",
    )(z, z, sgu_gain, w_spatial, b_spatial)


def _moba_kernel(q_ref, k_ref, v_ref, o_ref, kmh_ref, kml_ref, bias_ref, m_ref, l_ref, acc_ref):
    blk = MOBA_BLOCK
    n_blocks = k_ref.shape[0] // blk
    dn = (((1,), (1,)), ((), ()))

    kmean = jnp.mean(k_ref[...].astype(F32).reshape(n_blocks, blk, HEAD_DIM), axis=1)
    kmh = kmean.astype(BF16)
    kmh_ref[...] = kmh
    kml_ref[...] = (kmean - kmh.astype(F32)).astype(BF16)

    blk_iota = lax.broadcasted_iota(jnp.int32, (blk, n_blocks), 1)
    row = lax.broadcasted_iota(jnp.int32, (blk, blk), 0)
    col = lax.broadcasted_iota(jnp.int32, (blk, blk), 1)

    def query_block(qb, carry):
        rows = pl.ds(pl.multiple_of(qb * blk, blk), blk)
        q = q_ref[rows, :]

        gate = (lax.dot_general(q, kmh_ref[...], dn, preferred_element_type=F32)
                + lax.dot_general(q, kml_ref[...], dn, preferred_element_type=F32))
        past = blk_iota < qb
        g = jnp.where(past, gate, NEG_BIG)
        sel = jnp.zeros((blk, n_blocks), jnp.bool_)
        for _ in range(MOBA_TOPK):
            mx = jnp.max(g, axis=1, keepdims=True)
            first = jnp.min(jnp.where(g == mx, blk_iota, n_blocks), axis=1, keepdims=True)
            pick = blk_iota == first
            sel = jnp.logical_or(sel, pick)
            g = jnp.where(pick, -jnp.inf, g)
        bias_ref[...] = jnp.where(jnp.logical_and(sel, past), 0.0, NEG_BIG)

        s = lax.dot_general(q, k_ref[rows, :], dn, preferred_element_type=F32)
        s = jnp.where(col <= row, s, NEG_BIG)
        m0 = jnp.max(s, axis=1, keepdims=True)
        p = jnp.exp(s - m0)
        m_ref[...] = m0
        l_ref[...] = jnp.sum(p, axis=1, keepdims=True)
        acc_ref[...] = jnp.dot(p.astype(BF16), v_ref[rows, :], preferred_element_type=F32)

        def key_block(n, c):
            krows = pl.ds(pl.multiple_of(n * blk, blk), blk)
            bias_n = jnp.sum(jnp.where(blk_iota == n, bias_ref[...], 0.0), axis=1, keepdims=True)
            sn = lax.dot_general(q, k_ref[krows, :], dn, preferred_element_type=F32) + bias_n
            m_prev = m_ref[...]
            m_new = jnp.maximum(m_prev, jnp.max(sn, axis=1, keepdims=True))
            alpha = jnp.exp(m_prev - m_new)
            pn = jnp.exp(sn - m_new)
            l_ref[...] = alpha * l_ref[...] + jnp.sum(pn, axis=1, keepdims=True)
            acc_ref[...] = alpha * acc_ref[...] + jnp.dot(pn.astype(BF16), v_ref[krows, :],
                                                          preferred_element_type=F32)
            m_ref[...] = m_new
            return c

        lax.fori_loop(0, qb, key_block, 0)
        o_ref[rows, :] = (acc_ref[...] / l_ref[...]).astype(o_ref.dtype)
        return carry

    lax.fori_loop(0, n_blocks, query_block, 0)


def _moba_attention(q, kv):
    s = q.shape[0]
    n_blocks = s // MOBA_BLOCK
    head = lambda off: pl.BlockSpec((s, HEAD_DIM), lambda h: (0, h + off))
    return pl.pallas_call(
        _moba_kernel,
        grid=(N_HEADS,),
        in_specs=[head(0), head(0), head(N_HEADS)],
        out_specs=head(0),
        out_shape=jax.ShapeDtypeStruct(q.shape, BF16),
        scratch_shapes=[
            pltpu.VMEM((n_blocks, HEAD_DIM), BF16),
            pltpu.VMEM((n_blocks, HEAD_DIM), BF16),
            pltpu.VMEM((MOBA_BLOCK, n_blocks), F32),
            pltpu.VMEM((MOBA_BLOCK, 1), F32),
            pltpu.VMEM((MOBA_BLOCK, 1), F32),
            pltpu.VMEM((MOBA_BLOCK, HEAD_DIM), F32),
        ],
        compiler_params=_params(("parallel",)),
        name="moba_attention",
    )(q, kv, kv)


def _rmsnorm_kernel(x_ref, g_ref, o_ref):
    x = x_ref[...]
    inv = lax.rsqrt(jnp.mean(x * x, axis=-1, keepdims=True) + NORM_EPS)
    o_ref[...] = x_ref[...] * inv * g_ref[...]


def _rmsnorm(x, gain, *, tm=256):
    s, d = x.shape
    return pl.pallas_call(
        _rmsnorm_kernel,
        grid=(s // tm,),
        in_specs=[pl.BlockSpec((tm, d), lambda i: (i, 0)), pl.BlockSpec((1, d), lambda i: (0, 0))],
        out_specs=pl.BlockSpec((tm, d), lambda i: (i, 0)),
        out_shape=jax.ShapeDtypeStruct((s, d), F32),
        compiler_params=_params(("parallel",)),
        name="final_rmsnorm",
    )(x, gain)


def _rope_tables(s):
    inv_freq = 1.0 / (ROPE_THETA ** (jnp.arange(0, HEAD_DIM, 2, dtype=F32) / HEAD_DIM))
    ang = jnp.arange(s).astype(F32)[:, None] * inv_freq[None, :]
    cos = jnp.cos(jnp.concatenate([ang, ang], axis=-1))
    sin = jnp.sin(ang)
    return cos, jnp.concatenate([-sin, sin], axis=-1)


def _ffn_half_step(x, gain, w13, w2, layer, half):
    hmid = _swiglu_up(x, gain, w13, layer, half)
    return _matmul_residual(hmid, w2, (layer, half), x, alpha=MACARON_WEIGHT, tm=512, tn=512)


def kernel(x, ffn_norm, ffn_w13, ffn_w2, mix_norm, a_w_in, a_sgu_norm, a_w_spatial, a_b_spatial,
           a_w_out, kv_norm, w_kv, b_w_q, b_w_o, final_norm):
    batch, s, d = x.shape
    assert batch == 1 and d == N_HEADS * HEAD_DIM
    assert ffn_w13.shape[0] == 2 and a_w_in.shape[0] == 1 and b_w_q.shape[0] == 1

    w13 = ffn_w13.astype(BF16)
    w2 = ffn_w2.astype(BF16)
    w_in = a_w_in[0].astype(BF16)
    w_out = a_w_out[0].astype(BF16)
    w_kv_b = w_kv.astype(BF16)
    w_q = b_w_q[0].astype(BF16)
    w_o = b_w_o[0].astype(BF16)
    cos, sin_signed = _rope_tables(s)
    row = lambda g: g.reshape(1, -1)

    h = x[0]
    h = _ffn_half_step(h, row(ffn_norm[0, 0]), w13, w2, 0, 0)
    z = _gelu_in(h, row(mix_norm[0]), w_in)
    gated = _sgu_mix(z, row(a_sgu_norm[0]), a_w_spatial[0], a_b_spatial[0][:, :, None])
    h = _matmul_residual(gated, w_out, (), h, alpha=1.0, tm=1024, tn=512)
    h = _ffn_half_step(h, row(ffn_norm[0, 1]), w13, w2, 0, 1)

    kv = _rope_proj(h, row(kv_norm), w_kv_b, cos, sin_signed, rope_cols=d, scale=1.0)

    h = _ffn_half_step(h, row(ffn_norm[1, 0]), w13, w2, 1, 0)
    q = _rope_proj(h, row(mix_norm[1]), w_q, cos, sin_signed, rope_cols=d,
                   scale=1.0 / math.sqrt(HEAD_DIM))
    attn = _moba_attention(q, kv)
    h = _matmul_residual(attn, w_o, (), h, alpha=1.0, tm=1024, tn=512)
    h = _ffn_half_step(h, row(ffn_norm[1, 1]), w13, w2, 1, 1)
    return _rmsnorm(h, row(final_norm))[None]
```

```python
import functools
import math

import jax
import jax.numpy as jnp
from jax import lax
from jax.experimental import pallas as pl
from jax.experimental.pallas import tpu as pltpu

F32 = jnp.float32
BF16 = jnp.bfloat16

NORM_EPS = 1e-6
NEG_BIG = -1e30
MACARON_WEIGHT = 0.5
ROPE_THETA = 10000.0

SGU_CHUNK = 128
SGU_GROUPS = 32
N_HEADS = 32
HEAD_DIM = 128
MOBA_BLOCK = 256
MOBA_TOPK = 3
MOBA_TILE = 1024

LANES = 128
VMEM_LIMIT_BYTES = 56 * 1024 * 1024

NORM_ROWS = 128


def _params(semantics):
    return pltpu.CompilerParams(dimension_semantics=semantics, vmem_limit_bytes=VMEM_LIMIT_BYTES)


def _row_tile_spec(tm, d):
    return pl.BlockSpec((tm, d), lambda i, j: (i, 0), pipeline_mode=pl.Buffered(1))


def _mxu_weight(w_ref):
    w = w_ref[...]
    return w if w.dtype == BF16 else w.astype(BF16)


def _fill_normed(x_ref, g_ref, hn_ref):
    gain = g_ref[...]

    def body(r, carry):
        rows = pl.ds(pl.multiple_of(r * NORM_ROWS, NORM_ROWS), NORM_ROWS)
        xr = x_ref[rows, :]
        inv = lax.rsqrt(jnp.mean(xr * xr, axis=-1, keepdims=True) + NORM_EPS)
        hn_ref[rows, :] = (x_ref[rows, :] * inv * gain).astype(hn_ref.dtype)
        return carry

    lax.fori_loop(0, x_ref.shape[0] // NORM_ROWS, body, 0)


def _swiglu_up_kernel(x_ref, g_ref, wg_ref, wu_ref, o_ref, hn_ref):
    @pl.when(pl.program_id(1) == 0)
    def _():
        _fill_normed(x_ref, g_ref, hn_ref)

    gate = jnp.dot(hn_ref[...], _mxu_weight(wg_ref), preferred_element_type=F32)
    up = jnp.dot(hn_ref[...], _mxu_weight(wu_ref), preferred_element_type=F32)
    o_ref[...] = (jax.nn.silu(gate) * up).astype(o_ref.dtype)


def _swiglu_up(x, gain, w13, layer, half, *, tm=1024, tn=256):
    s, d = x.shape
    f = w13.shape[-1] // 2
    nj = f // tn
    return pl.pallas_call(
        _swiglu_up_kernel,
        grid=(s // tm, nj),
        in_specs=[
            _row_tile_spec(tm, d),
            pl.BlockSpec((1, d), lambda i, j: (0, 0)),
            pl.BlockSpec((None, None, d, tn), lambda i, j: (layer, half, 0, j)),
            pl.BlockSpec((None, None, d, tn), lambda i, j: (layer, half, 0, j + nj)),
        ],
        out_specs=pl.BlockSpec((tm, tn), lambda i, j: (i, j)),
        out_shape=jax.ShapeDtypeStruct((s, f), BF16),
        scratch_shapes=[pltpu.VMEM((tm, d), BF16)],
        compiler_params=_params(("parallel", "arbitrary")),
        name="swiglu_up",
    )(x, gain, w13, w13)


def _gelu_in_kernel(x_ref, g_ref, w_ref, o_ref, hn_ref):
    @pl.when(pl.program_id(1) == 0)
    def _():
        _fill_normed(x_ref, g_ref, hn_ref)

    z = jnp.dot(hn_ref[...], _mxu_weight(w_ref), preferred_element_type=F32)
    o_ref[...] = (0.5 * z * (1.0 + lax.erf(z * math.sqrt(0.5)))).astype(o_ref.dtype)


def _gelu_in(x, gain, w, *, tm=1024, tn=512):
    s, d = x.shape
    n = w.shape[-1]
    return pl.pallas_call(
        _gelu_in_kernel,
        grid=(s // tm, n // tn),
        in_specs=[
            _row_tile_spec(tm, d),
            pl.BlockSpec((1, d), lambda i, j: (0, 0)),
            pl.BlockSpec((d, tn), lambda i, j: (0, j)),
        ],
        out_specs=pl.BlockSpec((tm, tn), lambda i, j: (i, j)),
        out_shape=jax.ShapeDtypeStruct((s, n), BF16),
        scratch_shapes=[pltpu.VMEM((tm, d), BF16)],
        compiler_params=_params(("parallel", "arbitrary")),
        name="gelu_in",
    )(x, gain, w)


def _rope_store(acc, cos, sin_signed, o_ref, scale):
    for h in range(acc.shape[1] // HEAD_DIM):
        cols = slice(h * HEAD_DIM, (h + 1) * HEAD_DIM)
        t = acc[:, cols]
        out = t * cos + pltpu.roll(t, HEAD_DIM // 2, axis=1) * sin_signed
        if scale != 1.0:
            out = out * scale
        o_ref[:, cols] = out.astype(o_ref.dtype)


def _rope_proj_kernel(x_ref, g_ref, w_ref, cos_ref, sin_ref, o_ref, hn_ref, *, n_rope_tiles, scale):
    j = pl.program_id(1)

    @pl.when(j == 0)
    def _():
        _fill_normed(x_ref, g_ref, hn_ref)

    @pl.when(j < n_rope_tiles)
    def _():
        acc = jnp.dot(hn_ref[...], _mxu_weight(w_ref), preferred_element_type=F32)
        _rope_store(acc, cos_ref[...], sin_ref[...], o_ref, scale)

    @pl.when(j >= n_rope_tiles)
    def _():
        o_ref[...] = jnp.dot(hn_ref[...], _mxu_weight(w_ref),
                             preferred_element_type=F32).astype(o_ref.dtype)


def _rope_proj(x, gain, w, cos, sin_signed, *, rope_cols, scale, tm=1024, tn=512):
    s, d = x.shape
    n = w.shape[-1]
    kern = functools.partial(_rope_proj_kernel, n_rope_tiles=rope_cols // tn, scale=scale)
    return pl.pallas_call(
        kern,
        grid=(s // tm, n // tn),
        in_specs=[
            _row_tile_spec(tm, d),
            pl.BlockSpec((1, d), lambda i, j: (0, 0)),
            pl.BlockSpec((d, tn), lambda i, j: (0, j)),
            pl.BlockSpec((tm, HEAD_DIM), lambda i, j: (i, 0)),
            pl.BlockSpec((tm, HEAD_DIM), lambda i, j: (i, 0)),
        ],
        out_specs=pl.BlockSpec((tm, tn), lambda i, j: (i, j)),
        out_shape=jax.ShapeDtypeStruct((s, n), BF16),
        scratch_shapes=[pltpu.VMEM((tm, d), BF16)],
        compiler_params=_params(("parallel", "arbitrary")),
        name="rope_proj",
    )(x, gain, w, cos, sin_signed)


def _matmul_residual_kernel(a_ref, w_ref, r_ref, o_ref, *, alpha):
    y = jnp.dot(a_ref[...], _mxu_weight(w_ref), preferred_element_type=F32)
    if alpha != 1.0:
        y = alpha * y
    o_ref[...] = r_ref[...] + y


def _matmul_residual(a, w, w_index, res, *, alpha, tm, tn):
    s, k = a.shape
    n = w.shape[-1]
    lead = (None,) * len(w_index)
    return pl.pallas_call(
        functools.partial(_matmul_residual_kernel, alpha=alpha),
        grid=(s // tm, n // tn),
        in_specs=[
            pl.BlockSpec((tm, k), lambda i, j: (i, 0)),
            pl.BlockSpec(lead + (k, tn), lambda i, j: w_index + (0, j)),
            pl.BlockSpec((tm, tn), lambda i, j: (i, j)),
        ],
        out_specs=pl.BlockSpec((tm, tn), lambda i, j: (i, j)),
        out_shape=jax.ShapeDtypeStruct((s, n), F32),
        compiler_params=_params(("parallel", "parallel")),
        name="matmul_residual",
    )(a, w, res)


def _sgu_kernel(u_ref, v_ref, gn_ref, ws_ref, b_ref, o_ref, wsm_ref, vn_ref):
    @pl.when(pl.program_id(0) == 0)
    def _():
        t = lax.broadcasted_iota(jnp.int32, (SGU_CHUNK, SGU_CHUNK), 0)
        s = lax.broadcasted_iota(jnp.int32, (SGU_CHUNK, SGU_CHUNK), 1)
        for g in range(SGU_GROUPS):
            wsm_ref[g] = jnp.where(s <= t, ws_ref[g], 0.0).astype(wsm_ref.dtype)

    v = v_ref[...].astype(F32)
    inv = lax.rsqrt(jnp.mean(v * v, axis=-1, keepdims=True) + NORM_EPS)
    vn_ref[...] = (v * inv * gn_ref[...]).astype(vn_ref.dtype)
    for g in range(SGU_GROUPS):
        cols = slice(g * LANES, (g + 1) * LANES)
        mixed = jnp.dot(wsm_ref[g], vn_ref[:, cols], preferred_element_type=F32) + b_ref[g]
        o_ref[:, cols] = (u_ref[:, cols].astype(F32) * mixed).astype(o_ref.dtype)


def _sgu_mix(z, sgu_gain, w_spatial, b_spatial):
    s, two_e = z.shape
    e = two_e // 2
    return pl.pallas_call(
        _sgu_kernel,
        grid=(s // SGU_CHUNK,),
        in_specs=[
            pl.BlockSpec((SGU_CHUNK, e), lambda c: (c, 0)),
            pl.BlockSpec((SGU_CHUNK, e), lambda c: (c, 1)),
            pl.BlockSpec((1, e), lambda c: (0, 0)),
            pl.BlockSpec((SGU_GROUPS, SGU_CHUNK, SGU_CHUNK), lambda c: (0, 0, 0)),
            pl.BlockSpec((SGU_GROUPS, SGU_CHUNK, 1), lambda c: (0, 0, 0)),
        ],
        out_specs=pl.BlockSpec((SGU_CHUNK, e), lambda c: (c, 0)),
        out_shape=jax.ShapeDtypeStruct((s, e), BF16),
        scratch_shapes=[
            pltpu.VMEM((SGU_GROUPS, SGU_CHUNK, SGU_CHUNK), BF16),
            pltpu.VMEM((SGU_CHUNK, e), BF16),
        ],
        compiler_params=_params(("arbitrary",)),
        name="sgu_mix",
    )(z, z, sgu_gain, w_spatial, b_spatial)


def _moba_kernel(q_ref, k_ref, v_ref, o_ref, kaug_ref, vaug_ref, qaug_ref, kmean_ref, m_ref, acc_ref):
    blk, tile = MOBA_BLOCK, MOBA_TILE
    seq = k_ref.shape[0]
    n_blocks = seq // blk
    dn = (((1,), (1,)), ((), ()))

    kmean_ref[...] = jnp.zeros_like(kmean_ref)
    lane = lax.broadcasted_iota(jnp.int32, (blk, LANES), 1)

    def key_prep(b, carry):
        rows = pl.ds(pl.multiple_of(b * blk, blk), blk)
        kb = k_ref[rows, :]
        kaug_ref[rows, 0:HEAD_DIM] = kb
        kaug_ref[rows, HEAD_DIM:2 * HEAD_DIM] = (lane == b).astype(BF16)
        vaug_ref[rows, 0:HEAD_DIM] = v_ref[rows, :]
        vaug_ref[rows, HEAD_DIM:2 * HEAD_DIM] = jnp.ones((blk, LANES), BF16)
        kmean_ref[pl.ds(b, 1), :] = jnp.mean(kb.astype(F32), axis=0, keepdims=True)
        return carry

    lax.fori_loop(0, n_blocks, key_prep, 0)
    kmean = kmean_ref[...]
    kmean_hi = kmean.astype(BF16)
    kmean_lo = (kmean - kmean_hi.astype(F32)).astype(BF16)

    def query_tile(qt, carry):
        rows = pl.ds(pl.multiple_of(qt * tile, tile), tile)
        q = q_ref[rows, :]

        gate = (lax.dot_general(kmean_hi, q, dn, preferred_element_type=F32)
                + lax.dot_general(kmean_lo, q, dn, preferred_element_type=F32))
        n_iota = lax.broadcasted_iota(jnp.int32, (LANES, tile), 0)
        q_blk = qt * (tile // blk) + lax.broadcasted_iota(jnp.int32, (LANES, tile), 1) // blk
        past = n_iota < q_blk
        g = jnp.where(past, gate, NEG_BIG)
        sel = jnp.zeros((LANES, tile), jnp.bool_)
        for _ in range(MOBA_TOPK):
            mx = jnp.max(g, axis=0, keepdims=True)
            first = jnp.min(jnp.where(g == mx, n_iota, LANES), axis=0, keepdims=True)
            pick = n_iota == first
            sel = jnp.logical_or(sel, pick)
            g = jnp.where(pick, -jnp.inf, g)
        allowed = jnp.logical_or(jnp.logical_and(sel, past), n_iota == q_blk)
        mask = jnp.where(allowed, 0.0, NEG_BIG)
        qaug_ref[:, 0:HEAD_DIM] = q
        qaug_ref[:, HEAD_DIM:2 * HEAD_DIM] = mask.T.astype(BF16)

        q_groups = [slice(r * blk, (r + 1) * blk) for r in range(tile // blk)]

        row = lax.broadcasted_iota(jnp.int32, (blk, blk), 0)
        col = lax.broadcasted_iota(jnp.int32, (blk, blk), 1)
        seen = [pl.ds(pl.multiple_of(qt * tile, tile), (r + 1) * blk) for r in range(tile // blk)]
        scores = [lax.dot_general(qaug_ref[rr, :], kaug_ref[seen[r], :], dn,
                                  preferred_element_type=F32) for r, rr in enumerate(q_groups)]
        for r, rr in enumerate(q_groups):
            s = scores[r]
            s_own = jnp.where(col <= row, s[:, r * blk:], NEG_BIG)
            s = jnp.concatenate([s[:, :r * blk], s_own], axis=1) if r else s_own
            m0 = jnp.max(s, axis=1, keepdims=True)
            p = jnp.exp2(s - m0)
            m_ref[rr, :] = m0
            acc_ref[rr, :] = jnp.dot(p.astype(BF16), vaug_ref[seen[r], :],
                                     preferred_element_type=F32)

        def past_tile(t, c):
            krows = pl.ds(pl.multiple_of(t * tile, tile), tile)
            scores = [lax.dot_general(qaug_ref[rr, :], kaug_ref[krows, :], dn,
                                      preferred_element_type=F32) for rr in q_groups]
            for rr, sn in zip(q_groups, scores):
                m_prev = m_ref[rr, :]
                m_new = jnp.maximum(m_prev, jnp.max(sn, axis=1, keepdims=True))
                pn = jnp.exp2(sn - m_new)
                acc_ref[rr, :] = jnp.exp2(m_prev - m_new) * acc_ref[rr, :] + jnp.dot(
                    pn.astype(BF16), vaug_ref[krows, :], preferred_element_type=F32)
                m_ref[rr, :] = m_new
            return c

        lax.fori_loop(0, qt, past_tile, 0)
        acc = acc_ref[...]
        o_ref[rows, :] = (acc[:, :HEAD_DIM] / acc[:, HEAD_DIM:HEAD_DIM + 1]).astype(o_ref.dtype)
        return carry

    lax.fori_loop(0, seq // tile, query_tile, 0)


def _moba_attention(q, kv):
    s = q.shape[0]
    assert s % MOBA_TILE == 0 and s // MOBA_BLOCK <= LANES
    head = lambda off: pl.BlockSpec((s, HEAD_DIM), lambda h: (0, h + off))
    return pl.pallas_call(
        _moba_kernel,
        grid=(N_HEADS,),
        in_specs=[head(0), head(0), head(N_HEADS)],
        out_specs=head(0),
        out_shape=jax.ShapeDtypeStruct(q.shape, BF16),
        scratch_shapes=[
            pltpu.VMEM((s, 2 * HEAD_DIM), BF16),
            pltpu.VMEM((s, 2 * HEAD_DIM), BF16),
            pltpu.VMEM((MOBA_TILE, 2 * HEAD_DIM), BF16),
            pltpu.VMEM((LANES, HEAD_DIM), F32),
            pltpu.VMEM((MOBA_TILE, 1), F32),
            pltpu.VMEM((MOBA_TILE, 2 * HEAD_DIM), F32),
        ],
        compiler_params=_params(("parallel",)),
        name="moba_attention",
    )(q, kv, kv)


def _rmsnorm_kernel(x_ref, g_ref, o_ref):
    x = x_ref[...]
    inv = lax.rsqrt(jnp.mean(x * x, axis=-1, keepdims=True) + NORM_EPS)
    o_ref[...] = x_ref[...] * inv * g_ref[...]


def _rmsnorm(x, gain, *, tm=256):
    s, d = x.shape
    return pl.pallas_call(
        _rmsnorm_kernel,
        grid=(s // tm,),
        in_specs=[pl.BlockSpec((tm, d), lambda i: (i, 0)), pl.BlockSpec((1, d), lambda i: (0, 0))],
        out_specs=pl.BlockSpec((tm, d), lambda i: (i, 0)),
        out_shape=jax.ShapeDtypeStruct((s, d), F32),
        compiler_params=_params(("parallel",)),
        name="final_rmsnorm",
    )(x, gain)


def _rope_tables(s):
    inv_freq = 1.0 / (ROPE_THETA ** (jnp.arange(0, HEAD_DIM, 2, dtype=F32) / HEAD_DIM))
    ang = jnp.arange(s).astype(F32)[:, None] * inv_freq[None, :]
    cos = jnp.cos(jnp.concatenate([ang, ang], axis=-1))
    sin = jnp.sin(ang)
    return cos, jnp.concatenate([-sin, sin], axis=-1)


def _ffn_half_step(x, gain, w13, w2, layer, half):
    hmid = _swiglu_up(x, gain, w13, layer, half)
    return _matmul_residual(hmid, w2, (layer, half), x, alpha=MACARON_WEIGHT, tm=512, tn=512)


def kernel(x, ffn_norm, ffn_w13, ffn_w2, mix_norm, a_w_in, a_sgu_norm, a_w_spatial, a_b_spatial,
           a_w_out, kv_norm, w_kv, b_w_q, b_w_o, final_norm):
    batch, s, d = x.shape
    assert batch == 1 and d == N_HEADS * HEAD_DIM
    assert ffn_w13.shape[0] == 2 and a_w_in.shape[0] == 1 and b_w_q.shape[0] == 1

    w2 = ffn_w2.astype(BF16)
    cos, sin_signed = _rope_tables(s)
    row = lambda g: g.reshape(1, -1)

    h = x[0]
    h = _ffn_half_step(h, row(ffn_norm[0, 0]), ffn_w13, w2, 0, 0)
    z = _gelu_in(h, row(mix_norm[0]), a_w_in[0])
    gated = _sgu_mix(z, row(a_sgu_norm[0]), a_w_spatial[0], a_b_spatial[0][:, :, None])
    h = _matmul_residual(gated, a_w_out[0], (), h, alpha=1.0, tm=1024, tn=512)
    h = _ffn_half_step(h, row(ffn_norm[0, 1]), ffn_w13, w2, 0, 1)

    kv = _rope_proj(h, row(kv_norm), w_kv, cos, sin_signed, rope_cols=d, scale=1.0)

    h = _ffn_half_step(h, row(ffn_norm[1, 0]), ffn_w13, w2, 1, 0)
    q = _rope_proj(h, row(mix_norm[1]), b_w_q[0], cos, sin_signed, rope_cols=d,
                   scale=math.log2(math.e) / math.sqrt(HEAD_DIM))
    attn = _moba_attention(q, kv)
    h = _matmul_residual(attn, b_w_o[0], (), h, alpha=1.0, tm=1024, tn=512)
    h = _ffn_half_step(h, row(ffn_norm[1, 1]), ffn_w13, w2, 1, 1)
    return _rmsnorm(h, row(final_norm))[None]
```

```python
import functools
import math

import jax
import jax.numpy as jnp
from jax import lax
from jax.experimental import pallas as pl
from jax.experimental.pallas import tpu as pltpu

F32 = jnp.float32
BF16 = jnp.bfloat16

NORM_EPS = 1e-6
NEG_BIG = -1e30
MACARON_WEIGHT = 0.5
ROPE_THETA = 10000.0

SGU_CHUNK = 128
SGU_GROUPS = 32
N_HEADS = 32
HEAD_DIM = 128
MOBA_BLOCK = 256
MOBA_TOPK = 3
MOBA_TILE = 1024

LANES = 128
VMEM_LIMIT_BYTES = 56 * 1024 * 1024


def _params(semantics):
    return pltpu.CompilerParams(dimension_semantics=semantics, vmem_limit_bytes=VMEM_LIMIT_BYTES)


def _row_tile_spec(tm, d):
    return pl.BlockSpec((tm, d), lambda i, j: (i, 0), pipeline_mode=pl.Buffered(1))


def _mxu_weight(w_ref):
    w = w_ref[...]
    return w if w.dtype == BF16 else w.astype(BF16)


def _norm_prep_kernel(x_ref, g_ref, xb_ref, inv_ref):
    x = x_ref[...]
    inv_ref[...] = lax.rsqrt(jnp.mean(x * x, axis=-1, keepdims=True) + NORM_EPS)
    xb_ref[...] = (x_ref[...] * g_ref[...]).astype(xb_ref.dtype)


def _norm_prep(x, gain, *, tm=256):
    s, d = x.shape
    return pl.pallas_call(
        _norm_prep_kernel,
        grid=(s // tm,),
        in_specs=[pl.BlockSpec((tm, d), lambda i: (i, 0)), pl.BlockSpec((1, d), lambda i: (0, 0))],
        out_specs=[pl.BlockSpec((tm, d), lambda i: (i, 0)), pl.BlockSpec((tm, 1), lambda i: (i, 0))],
        out_shape=[jax.ShapeDtypeStruct((s, d), BF16), jax.ShapeDtypeStruct((s, 1), F32)],
        compiler_params=_params(("parallel",)),
        name="norm_prep",
    )(x, gain)


def _swiglu_up_kernel(xb_ref, inv_ref, wg_ref, wu_ref, w2_ref, o_ref, w2o_ref):
    tn = wg_ref.shape[1]
    w = jnp.concatenate([_mxu_weight(wg_ref), _mxu_weight(wu_ref)], axis=1)
    gu = jnp.dot(xb_ref[...], w, preferred_element_type=F32) * inv_ref[...]
    o_ref[...] = (jax.nn.silu(gu[:, :tn]) * gu[:, tn:]).astype(o_ref.dtype)
    w2o_ref[...] = w2_ref[...].astype(w2o_ref.dtype)


def _swiglu_up(xb, inv, w13, w2, layer, half, *, tm=2048, tn=256):
    s, d = xb.shape
    f = w13.shape[-1] // 2
    nj = f // tn
    steps = (s // tm) * nj
    w2_rows = f // steps
    assert w2_rows * steps == f and w2_rows % 16 == 0
    return pl.pallas_call(
        _swiglu_up_kernel,
        grid=(s // tm, nj),
        in_specs=[
            _row_tile_spec(tm, d),
            pl.BlockSpec((tm, 1), lambda i, j: (i, 0)),
            pl.BlockSpec((None, None, d, tn), lambda i, j: (layer, half, 0, j)),
            pl.BlockSpec((None, None, d, tn), lambda i, j: (layer, half, 0, j + nj)),
            pl.BlockSpec((None, None, w2_rows, d), lambda i, j: (layer, half, i * nj + j, 0)),
        ],
        out_specs=[
            pl.BlockSpec((tm, tn), lambda i, j: (i, j)),
            pl.BlockSpec((w2_rows, d), lambda i, j: (i * nj + j, 0)),
        ],
        out_shape=[jax.ShapeDtypeStruct((s, f), BF16), jax.ShapeDtypeStruct((f, d), BF16)],
        compiler_params=_params(("parallel", "parallel")),
        name="swiglu_up",
    )(xb, inv, w13, w13, w2)


def _gelu_in_kernel(xb_ref, inv_ref, w_ref, o_ref):
    z = jnp.dot(xb_ref[...], _mxu_weight(w_ref), preferred_element_type=F32) * inv_ref[...]
    o_ref[...] = (0.5 * z * (1.0 + lax.erf(z * math.sqrt(0.5)))).astype(o_ref.dtype)


def _gelu_in(xb, inv, w, *, tm=2048, tn=512):
    s, d = xb.shape
    n = w.shape[-1]
    return pl.pallas_call(
        _gelu_in_kernel,
        grid=(s // tm, n // tn),
        in_specs=[
            _row_tile_spec(tm, d),
            pl.BlockSpec((tm, 1), lambda i, j: (i, 0)),
            pl.BlockSpec((d, tn), lambda i, j: (0, j)),
        ],
        out_specs=pl.BlockSpec((tm, tn), lambda i, j: (i, j)),
        out_shape=jax.ShapeDtypeStruct((s, n), BF16),
        compiler_params=_params(("parallel", "parallel")),
        name="gelu_in",
    )(xb, inv, w)


def _rope_store(acc, cos, sin_signed, o_ref):
    for h in range(acc.shape[1] // HEAD_DIM):
        cols = slice(h * HEAD_DIM, (h + 1) * HEAD_DIM)
        t = acc[:, cols]
        out = t * cos + pltpu.roll(t, HEAD_DIM // 2, axis=1) * sin_signed
        o_ref[:, cols] = out.astype(o_ref.dtype)


def _rope_proj_kernel(xb_ref, inv_ref, w_ref, cos_ref, sin_ref, o_ref, *, n_rope_tiles, scale):
    j = pl.program_id(1)
    row_scale = inv_ref[...] if scale == 1.0 else inv_ref[...] * scale

    @pl.when(j < n_rope_tiles)
    def _():
        acc = jnp.dot(xb_ref[...], _mxu_weight(w_ref), preferred_element_type=F32) * row_scale
        _rope_store(acc, cos_ref[...], sin_ref[...], o_ref)

    @pl.when(j >= n_rope_tiles)
    def _():
        o_ref[...] = (jnp.dot(xb_ref[...], _mxu_weight(w_ref), preferred_element_type=F32)
                      * inv_ref[...]).astype(o_ref.dtype)


def _rope_proj(xb, inv, w, cos, sin_signed, *, rope_cols, scale, tm=2048, tn=512):
    s, d = xb.shape
    n = w.shape[-1]
    assert rope_cols == n or scale == 1.0
    kern = functools.partial(_rope_proj_kernel, n_rope_tiles=rope_cols // tn, scale=scale)
    return pl.pallas_call(
        kern,
        grid=(s // tm, n // tn),
        in_specs=[
            _row_tile_spec(tm, d),
            pl.BlockSpec((tm, 1), lambda i, j: (i, 0)),
            pl.BlockSpec((d, tn), lambda i, j: (0, j)),
            pl.BlockSpec((tm, HEAD_DIM), lambda i, j: (i, 0)),
            pl.BlockSpec((tm, HEAD_DIM), lambda i, j: (i, 0)),
        ],
        out_specs=pl.BlockSpec((tm, tn), lambda i, j: (i, j)),
        out_shape=jax.ShapeDtypeStruct((s, n), BF16),
        compiler_params=_params(("parallel", "parallel")),
        name="rope_proj",
    )(xb, inv, w, cos, sin_signed)


def _matmul_residual_kernel(a_ref, w_ref, r_ref, *rest, alpha, n_gains, d):
    gain_refs = rest[:n_gains]
    o_ref = rest[n_gains]
    xb_refs = rest[n_gains + 1:2 * n_gains + 1]
    inv_ref = rest[2 * n_gains + 1]
    j = pl.program_id(1)

    y = jnp.dot(a_ref[...], _mxu_weight(w_ref), preferred_element_type=F32)
    if alpha != 1.0:
        y = alpha * y
    y = r_ref[...] + y
    o_ref[...] = y
    for g_ref, xb_ref in zip(gain_refs, xb_refs):
        xb_ref[...] = (y * g_ref[...]).astype(xb_ref.dtype)

    ss = jnp.sum(y * y, axis=-1, keepdims=True)

    @pl.when(j == 0)
    def _():
        inv_ref[...] = ss

    @pl.when(j > 0)
    def _():
        inv_ref[...] += ss

    @pl.when(j == pl.num_programs(1) - 1)
    def _():
        inv_ref[...] = lax.rsqrt(inv_ref[...] * (1.0 / d) + NORM_EPS)


def _matmul_residual(a, w, res, gains, *, alpha, tm, tn):
    s, k = a.shape
    n = w.shape[-1]
    n_gains = len(gains)
    tile = lambda dtype: (pl.BlockSpec((tm, tn), lambda i, j: (i, j)), jax.ShapeDtypeStruct((s, n), dtype))
    outs = [tile(F32)] + [tile(BF16)] * n_gains
    outs.append((pl.BlockSpec((tm, 1), lambda i, j: (i, 0)), jax.ShapeDtypeStruct((s, 1), F32)))
    return pl.pallas_call(
        functools.partial(_matmul_residual_kernel, alpha=alpha, n_gains=n_gains, d=n),
        grid=(s // tm, n // tn),
        in_specs=[
            pl.BlockSpec((tm, k), lambda i, j: (i, 0)),
            pl.BlockSpec((k, tn), lambda i, j: (0, j)),
            pl.BlockSpec((tm, tn), lambda i, j: (i, j)),
        ] + [pl.BlockSpec((1, tn), lambda i, j: (0, j))] * n_gains,
        out_specs=[spec for spec, _ in outs],
        out_shape=[shape for _, shape in outs],
        compiler_params=_params(("parallel", "arbitrary")),
        name="matmul_residual",
    )(a, w, res, *gains)


def _sgu_kernel(u_ref, v_ref, gn_ref, ws_ref, b_ref, o_ref, wsm_ref, vn_ref):
    @pl.when(pl.program_id(0) == 0)
    def _():
        t = lax.broadcasted_iota(jnp.int32, (SGU_CHUNK, SGU_CHUNK), 0)
        s = lax.broadcasted_iota(jnp.int32, (SGU_CHUNK, SGU_CHUNK), 1)
        for g in range(SGU_GROUPS):
            wsm_ref[g] = jnp.where(s <= t, ws_ref[g], 0.0).astype(wsm_ref.dtype)

    v = v_ref[...].astype(F32)
    inv = lax.rsqrt(jnp.mean(v * v, axis=-1, keepdims=True) + NORM_EPS)
    vn_ref[...] = (v * inv * gn_ref[...]).astype(vn_ref.dtype)
    for g in range(SGU_GROUPS):
        cols = slice(g * LANES, (g + 1) * LANES)
        mixed = jnp.dot(wsm_ref[g], vn_ref[:, cols], preferred_element_type=F32) + b_ref[g]
        o_ref[:, cols] = (u_ref[:, cols].astype(F32) * mixed).astype(o_ref.dtype)


def _sgu_mix(z, sgu_gain, w_spatial, b_spatial):
    s, two_e = z.shape
    e = two_e // 2
    return pl.pallas_call(
        _sgu_kernel,
        grid=(s // SGU_CHUNK,),
        in_specs=[
            pl.BlockSpec((SGU_CHUNK, e), lambda c: (c, 0)),
            pl.BlockSpec((SGU_CHUNK, e), lambda c: (c, 1)),
            pl.BlockSpec((1, e), lambda c: (0, 0)),
            pl.BlockSpec((SGU_GROUPS, SGU_CHUNK, SGU_CHUNK), lambda c: (0, 0, 0)),
            pl.BlockSpec((SGU_GROUPS, SGU_CHUNK, 1), lambda c: (0, 0, 0)),
        ],
        out_specs=pl.BlockSpec((SGU_CHUNK, e), lambda c: (c, 0)),
        out_shape=jax.ShapeDtypeStruct((s, e), BF16),
        scratch_shapes=[
            pltpu.VMEM((SGU_GROUPS, SGU_CHUNK, SGU_CHUNK), BF16),
            pltpu.VMEM((SGU_CHUNK, e), BF16),
        ],
        compiler_params=_params(("arbitrary",)),
        name="sgu_mix",
    )(z, z, sgu_gain, w_spatial, b_spatial)


def _moba_kernel(q_ref, k_ref, v_ref, o_ref, kaug_ref, vaug_ref, qaug_ref, kmean_ref, m_ref, acc_ref):
    blk, tile = MOBA_BLOCK, MOBA_TILE
    seq = k_ref.shape[0]
    n_blocks = seq // blk
    dn = (((1,), (1,)), ((), ()))

    kmean_ref[...] = jnp.zeros_like(kmean_ref)
    lane = lax.broadcasted_iota(jnp.int32, (blk, LANES), 1)

    def key_prep(b, carry):
        rows = pl.ds(pl.multiple_of(b * blk, blk), blk)
        kb = k_ref[rows, :]
        kaug_ref[rows, 0:HEAD_DIM] = kb
        kaug_ref[rows, HEAD_DIM:2 * HEAD_DIM] = (lane == b).astype(BF16)
        vaug_ref[rows, 0:HEAD_DIM] = v_ref[rows, :]
        vaug_ref[rows, HEAD_DIM:2 * HEAD_DIM] = jnp.ones((blk, LANES), BF16)
        kmean_ref[pl.ds(b, 1), :] = jnp.mean(kb.astype(F32), axis=0, keepdims=True)
        return carry

    lax.fori_loop(0, n_blocks, key_prep, 0)
    kmean = kmean_ref[...]
    kmean_hi = kmean.astype(BF16)
    kmean_lo = (kmean - kmean_hi.astype(F32)).astype(BF16)

    def query_tile(qt, carry):
        rows = pl.ds(pl.multiple_of(qt * tile, tile), tile)
        q = q_ref[rows, :]

        gate = (lax.dot_general(kmean_hi, q, dn, preferred_element_type=F32)
                + lax.dot_general(kmean_lo, q, dn, preferred_element_type=F32))
        n_iota = lax.broadcasted_iota(jnp.int32, (LANES, tile), 0)
        q_blk = qt * (tile // blk) + lax.broadcasted_iota(jnp.int32, (LANES, tile), 1) // blk
        past = n_iota < q_blk
        g = jnp.where(past, gate, NEG_BIG)
        sel = jnp.zeros((LANES, tile), jnp.bool_)
        for _ in range(MOBA_TOPK):
            mx = jnp.max(g, axis=0, keepdims=True)
            first = jnp.min(jnp.where(g == mx, n_iota, LANES), axis=0, keepdims=True)
            pick = n_iota == first
            sel = jnp.logical_or(sel, pick)
            g = jnp.where(pick, -jnp.inf, g)
        allowed = jnp.logical_or(jnp.logical_and(sel, past), n_iota == q_blk)
        mask = jnp.where(allowed, 0.0, NEG_BIG)
        qaug_ref[:, 0:HEAD_DIM] = q
        qaug_ref[:, HEAD_DIM:2 * HEAD_DIM] = mask.T.astype(BF16)

        q_groups = [slice(r * blk, (r + 1) * blk) for r in range(tile // blk)]

        row = lax.broadcasted_iota(jnp.int32, (blk, blk), 0)
        col = lax.broadcasted_iota(jnp.int32, (blk, blk), 1)
        seen = [pl.ds(pl.multiple_of(qt * tile, tile), (r + 1) * blk) for r in range(tile // blk)]
        scores = [lax.dot_general(qaug_ref[rr, :], kaug_ref[seen[r], :], dn,
                                  preferred_element_type=F32) for r, rr in enumerate(q_groups)]
        for r, rr in enumerate(q_groups):
            s = scores[r]
            s_own = jnp.where(col <= row, s[:, r * blk:], NEG_BIG)
            s = jnp.concatenate([s[:, :r * blk], s_own], axis=1) if r else s_own
            m0 = jnp.max(s, axis=1, keepdims=True)
            p = jnp.exp2(s - m0)
            m_ref[rr, :] = m0
            acc_ref[rr, :] = jnp.dot(p.astype(BF16), vaug_ref[seen[r], :],
                                     preferred_element_type=F32)

        def past_tile(t, c):
            krows = pl.ds(pl.multiple_of(t * tile, tile), tile)
            scores = [lax.dot_general(qaug_ref[rr, :], kaug_ref[krows, :], dn,
                                      preferred_element_type=F32) for rr in q_groups]
            for rr, sn in zip(q_groups, scores):
                m_prev = m_ref[rr, :]
                m_new = jnp.maximum(m_prev, jnp.max(sn, axis=1, keepdims=True))
                pn = jnp.exp2(sn - m_new)
                acc_ref[rr, :] = jnp.exp2(m_prev - m_new) * acc_ref[rr, :] + jnp.dot(
                    pn.astype(BF16), vaug_ref[krows, :], preferred_element_type=F32)
                m_ref[rr, :] = m_new
            return c

        lax.fori_loop(0, qt, past_tile, 0)
        acc = acc_ref[...]
        o_ref[rows, :] = (acc[:, :HEAD_DIM] / acc[:, HEAD_DIM:HEAD_DIM + 1]).astype(o_ref.dtype)
        return carry

    lax.fori_loop(0, seq // tile, query_tile, 0)


def _moba_attention(q, kv):
    s = q.shape[0]
    assert s % MOBA_TILE == 0 and s // MOBA_BLOCK <= LANES
    head = lambda off: pl.BlockSpec((s, HEAD_DIM), lambda h: (0, h + off))
    return pl.pallas_call(
        _moba_kernel,
        grid=(N_HEADS,),
        in_specs=[head(0), head(0), head(N_HEADS)],
        out_specs=head(0),
        out_shape=jax.ShapeDtypeStruct(q.shape, BF16),
        scratch_shapes=[
            pltpu.VMEM((s, 2 * HEAD_DIM), BF16),
            pltpu.VMEM((s, 2 * HEAD_DIM), BF16),
            pltpu.VMEM((MOBA_TILE, 2 * HEAD_DIM), BF16),
            pltpu.VMEM((LANES, HEAD_DIM), F32),
            pltpu.VMEM((MOBA_TILE, 1), F32),
            pltpu.VMEM((MOBA_TILE, 2 * HEAD_DIM), F32),
        ],
        compiler_params=_params(("parallel",)),
        name="moba_attention",
    )(q, kv, kv)


def _scale_rows_kernel(x_ref, inv_ref, g_ref, o_ref):
    o_ref[...] = x_ref[...] * inv_ref[...] * g_ref[...]


def _scale_rows(x, inv, gain, *, tm=256):
    s, d = x.shape
    return pl.pallas_call(
        _scale_rows_kernel,
        grid=(s // tm,),
        in_specs=[
            pl.BlockSpec((tm, d), lambda i: (i, 0)),
            pl.BlockSpec((tm, 1), lambda i: (i, 0)),
            pl.BlockSpec((1, d), lambda i: (0, 0)),
        ],
        out_specs=pl.BlockSpec((tm, d), lambda i: (i, 0)),
        out_shape=jax.ShapeDtypeStruct((s, d), F32),
        compiler_params=_params(("parallel",)),
        name="final_rmsnorm",
    )(x, inv, gain)


def _rope_tables(s):
    inv_freq = 1.0 / (ROPE_THETA ** (jnp.arange(0, HEAD_DIM, 2, dtype=F32) / HEAD_DIM))
    ang = jnp.arange(s).astype(F32)[:, None] * inv_freq[None, :]
    cos = jnp.cos(jnp.concatenate([ang, ang], axis=-1))
    sin = jnp.sin(ang)
    return cos, jnp.concatenate([-sin, sin], axis=-1)


def _ffn_half_step(x, xb, inv, w13, w2, layer, half, next_gains):
    hmid, w2_bf16 = _swiglu_up(xb, inv, w13, w2, layer, half)
    return _matmul_residual(hmid, w2_bf16, x, next_gains, alpha=MACARON_WEIGHT, tm=512, tn=512)


def kernel(x, ffn_norm, ffn_w13, ffn_w2, mix_norm, a_w_in, a_sgu_norm, a_w_spatial, a_b_spatial,
           a_w_out, kv_norm, w_kv, b_w_q, b_w_o, final_norm):
    batch, s, d = x.shape
    assert batch == 1 and d == N_HEADS * HEAD_DIM
    assert ffn_w13.shape[0] == 2 and a_w_in.shape[0] == 1 and b_w_q.shape[0] == 1

    cos, sin_signed = _rope_tables(s)
    row = lambda g: g.reshape(1, -1)

    h = x[0]
    hb, inv = _norm_prep(h, row(ffn_norm[0, 0]))
    h, hb, inv = _ffn_half_step(h, hb, inv, ffn_w13, ffn_w2, 0, 0, [row(mix_norm[0])])
    z = _gelu_in(hb, inv, a_w_in[0])
    gated = _sgu_mix(z, row(a_sgu_norm[0]), a_w_spatial[0], a_b_spatial[0][:, :, None])
    h, hb, inv = _matmul_residual(gated, a_w_out[0], h, [row(ffn_norm[0, 1])], alpha=1.0,
                                  tm=1024, tn=512)
    h, hb_kv, hb, inv = _ffn_half_step(h, hb, inv, ffn_w13, ffn_w2, 0, 1,
                                       [row(kv_norm), row(ffn_norm[1, 0])])

    kv = _rope_proj(hb_kv, inv, w_kv, cos, sin_signed, rope_cols=d, scale=1.0)

    h, hb, inv = _ffn_half_step(h, hb, inv, ffn_w13, ffn_w2, 1, 0, [row(mix_norm[1])])
    q = _rope_proj(hb, inv, b_w_q[0], cos, sin_signed, rope_cols=d,
                   scale=math.log2(math.e) / math.sqrt(HEAD_DIM))
    attn = _moba_attention(q, kv)
    h, hb, inv = _matmul_residual(attn, b_w_o[0], h, [row(ffn_norm[1, 1])], alpha=1.0,
                                  tm=1024, tn=512)
    h, inv = _ffn_half_step(h, hb, inv, ffn_w13, ffn_w2, 1, 1, [])
    return _scale_rows(h, inv, row(final_norm))[None]
```

```python
import functools
import math

import jax
import jax.numpy as jnp
import numpy as np
from jax import lax
from jax.experimental import pallas as pl
from jax.experimental.pallas import tpu as pltpu

F32 = jnp.float32
BF16 = jnp.bfloat16

NORM_EPS = 1e-6
NEG_BIG = -1e30
MACARON_WEIGHT = 0.5
ROPE_THETA = 10000.0

SGU_CHUNK = 128
SGU_GROUPS = 32
N_HEADS = 32
HEAD_DIM = 128
MOBA_BLOCK = 256
MOBA_TOPK = 3
MOBA_TILE = 2048
MOBA_KEY_TILE = 1024

LANES = 128
VMEM_LIMIT_BYTES = 56 * 1024 * 1024


def _params(semantics):
    return pltpu.CompilerParams(dimension_semantics=semantics, vmem_limit_bytes=VMEM_LIMIT_BYTES)


def _row_tile_spec(tm, d):
    return pl.BlockSpec((tm, d), lambda i, j: (i, 0), pipeline_mode=pl.Buffered(1))


def _mxu_weight(w_ref):
    w = w_ref[...]
    return w if w.dtype == BF16 else w.astype(BF16)


def _norm_prep_kernel(x_ref, g_ref, xb_ref, inv_ref):
    x = x_ref[...]
    inv_ref[...] = lax.rsqrt(jnp.mean(x * x, axis=-1, keepdims=True) + NORM_EPS)
    xb_ref[...] = (x_ref[...] * g_ref[...]).astype(xb_ref.dtype)


def _norm_prep(x, gain, *, tm=256):
    s, d = x.shape
    return pl.pallas_call(
        _norm_prep_kernel,
        grid=(s // tm,),
        in_specs=[pl.BlockSpec((tm, d), lambda i: (i, 0)), pl.BlockSpec((1, d), lambda i: (0, 0))],
        out_specs=[pl.BlockSpec((tm, d), lambda i: (i, 0)), pl.BlockSpec((tm, 1), lambda i: (i, 0))],
        out_shape=[jax.ShapeDtypeStruct((s, d), BF16), jax.ShapeDtypeStruct((s, 1), F32)],
        compiler_params=_params(("parallel",)),
        name="norm_prep",
    )(x, gain)


def _swiglu_up_kernel(xb_ref, inv_ref, wg_ref, wu_ref, w2_ref, o_ref, w2o_ref):
    tn = wg_ref.shape[1]
    w = jnp.concatenate([_mxu_weight(wg_ref), _mxu_weight(wu_ref)], axis=1)
    gu = jnp.dot(xb_ref[...], w, preferred_element_type=F32) * inv_ref[...]
    o_ref[...] = (jax.nn.silu(gu[:, :tn]) * gu[:, tn:]).astype(o_ref.dtype)
    w2o_ref[...] = w2_ref[...].astype(w2o_ref.dtype)


def _swiglu_up(xb, inv, w13, w2, layer, half, *, tm=2048, tn=256):
    s, d = xb.shape
    f = w13.shape[-1] // 2
    nj = f // tn
    steps = (s // tm) * nj
    w2_rows = f // steps
    assert w2_rows * steps == f and w2_rows % 16 == 0
    return pl.pallas_call(
        _swiglu_up_kernel,
        grid=(s // tm, nj),
        in_specs=[
            _row_tile_spec(tm, d),
            pl.BlockSpec((tm, 1), lambda i, j: (i, 0)),
            pl.BlockSpec((None, None, d, tn), lambda i, j: (layer, half, 0, j)),
            pl.BlockSpec((None, None, d, tn), lambda i, j: (layer, half, 0, j + nj)),
            pl.BlockSpec((None, None, w2_rows, d), lambda i, j: (layer, half, i * nj + j, 0)),
        ],
        out_specs=[
            pl.BlockSpec((tm, tn), lambda i, j: (i, j)),
            pl.BlockSpec((w2_rows, d), lambda i, j: (i * nj + j, 0)),
        ],
        out_shape=[jax.ShapeDtypeStruct((s, f), BF16), jax.ShapeDtypeStruct((f, d), BF16)],
        compiler_params=_params(("parallel", "parallel")),
        name="swiglu_up",
    )(xb, inv, w13, w13, w2)


def _gelu_in_kernel(xb_ref, inv_ref, w_ref, o_ref):
    z = jnp.dot(xb_ref[...], _mxu_weight(w_ref), preferred_element_type=F32) * inv_ref[...]
    o_ref[...] = (0.5 * z * (1.0 + lax.erf(z * math.sqrt(0.5)))).astype(o_ref.dtype)


def _gelu_in(xb, inv, w, *, tm=2048, tn=512):
    s, d = xb.shape
    n = w.shape[-1]
    return pl.pallas_call(
        _gelu_in_kernel,
        grid=(s // tm, n // tn),
        in_specs=[
            _row_tile_spec(tm, d),
            pl.BlockSpec((tm, 1), lambda i, j: (i, 0)),
            pl.BlockSpec((d, tn), lambda i, j: (0, j)),
        ],
        out_specs=pl.BlockSpec((tm, tn), lambda i, j: (i, j)),
        out_shape=jax.ShapeDtypeStruct((s, n), BF16),
        compiler_params=_params(("parallel", "parallel")),
        name="gelu_in",
    )(xb, inv, w)


def _rope_store(acc, cos, sin_signed, o_ref):
    for h in range(acc.shape[1] // HEAD_DIM):
        cols = slice(h * HEAD_DIM, (h + 1) * HEAD_DIM)
        t = acc[:, cols]
        out = t * cos + pltpu.roll(t, HEAD_DIM // 2, axis=1) * sin_signed
        o_ref[:, cols] = out.astype(o_ref.dtype)


def _rope_proj_kernel(xb_ref, inv_ref, w_ref, cos_ref, sin_ref, o_ref, *, n_rope_tiles, scale):
    j = pl.program_id(1)
    row_scale = inv_ref[...] if scale == 1.0 else inv_ref[...] * scale

    @pl.when(j < n_rope_tiles)
    def _():
        acc = jnp.dot(xb_ref[...], _mxu_weight(w_ref), preferred_element_type=F32) * row_scale
        _rope_store(acc, cos_ref[...], sin_ref[...], o_ref)

    @pl.when(j >= n_rope_tiles)
    def _():
        o_ref[...] = (jnp.dot(xb_ref[...], _mxu_weight(w_ref), preferred_element_type=F32)
                      * inv_ref[...]).astype(o_ref.dtype)


def _rope_proj(xb, inv, w, cos, sin_signed, *, rope_cols, scale, tm=2048, tn=512):
    s, d = xb.shape
    n = w.shape[-1]
    assert rope_cols == n or scale == 1.0
    kern = functools.partial(_rope_proj_kernel, n_rope_tiles=rope_cols // tn, scale=scale)
    return pl.pallas_call(
        kern,
        grid=(s // tm, n // tn),
        in_specs=[
            _row_tile_spec(tm, d),
            pl.BlockSpec((tm, 1), lambda i, j: (i, 0)),
            pl.BlockSpec((d, tn), lambda i, j: (0, j)),
            pl.BlockSpec((tm, HEAD_DIM), lambda i, j: (i, 0)),
            pl.BlockSpec((tm, HEAD_DIM), lambda i, j: (i, 0)),
        ],
        out_specs=pl.BlockSpec((tm, tn), lambda i, j: (i, j)),
        out_shape=jax.ShapeDtypeStruct((s, n), BF16),
        compiler_params=_params(("parallel", "parallel")),
        name="rope_proj",
    )(xb, inv, w, cos, sin_signed)


def _matmul_residual_kernel(a_ref, w_ref, r_ref, *rest, alpha, n_gains, d):
    gain_refs = rest[:n_gains]
    o_ref = rest[n_gains]
    xb_refs = rest[n_gains + 1:2 * n_gains + 1]
    inv_ref = rest[2 * n_gains + 1]
    j = pl.program_id(1)

    y = jnp.dot(a_ref[...], _mxu_weight(w_ref), preferred_element_type=F32)
    if alpha != 1.0:
        y = alpha * y
    y = r_ref[...] + y
    o_ref[...] = y
    for g_ref, xb_ref in zip(gain_refs, xb_refs):
        xb_ref[...] = (y * g_ref[...]).astype(xb_ref.dtype)

    ss = jnp.sum(y * y, axis=-1, keepdims=True)

    @pl.when(j == 0)
    def _():
        inv_ref[...] = ss

    @pl.when(j > 0)
    def _():
        inv_ref[...] += ss

    @pl.when(j == pl.num_programs(1) - 1)
    def _():
        inv_ref[...] = lax.rsqrt(inv_ref[...] * (1.0 / d) + NORM_EPS)


def _matmul_residual(a, w, res, gains, *, alpha, tm, tn):
    s, k = a.shape
    n = w.shape[-1]
    n_gains = len(gains)
    tile = lambda dtype: (pl.BlockSpec((tm, tn), lambda i, j: (i, j)), jax.ShapeDtypeStruct((s, n), dtype))
    outs = [tile(F32)] + [tile(BF16)] * n_gains
    outs.append((pl.BlockSpec((tm, 1), lambda i, j: (i, 0)), jax.ShapeDtypeStruct((s, 1), F32)))
    return pl.pallas_call(
        functools.partial(_matmul_residual_kernel, alpha=alpha, n_gains=n_gains, d=n),
        grid=(s // tm, n // tn),
        in_specs=[
            pl.BlockSpec((tm, k), lambda i, j: (i, 0)),
            pl.BlockSpec((k, tn), lambda i, j: (0, j)),
            pl.BlockSpec((tm, tn), lambda i, j: (i, j)),
        ] + [pl.BlockSpec((1, tn), lambda i, j: (0, j))] * n_gains,
        out_specs=[spec for spec, _ in outs],
        out_shape=[shape for _, shape in outs],
        compiler_params=_params(("parallel", "arbitrary")),
        name="matmul_residual",
    )(a, w, res, *gains)


def _sgu_kernel(u_ref, v_ref, gn_ref, ws_ref, b_ref, o_ref, wsm_ref, vn_ref):
    @pl.when(pl.program_id(0) == 0)
    def _():
        t = lax.broadcasted_iota(jnp.int32, (SGU_CHUNK, SGU_CHUNK), 0)
        s = lax.broadcasted_iota(jnp.int32, (SGU_CHUNK, SGU_CHUNK), 1)
        for g in range(SGU_GROUPS):
            wsm_ref[g] = jnp.where(s <= t, ws_ref[g], 0.0).astype(wsm_ref.dtype)

    def chunk(c, carry):
        rows = pl.ds(pl.multiple_of(c * SGU_CHUNK, SGU_CHUNK), SGU_CHUNK)
        v = v_ref[rows, :].astype(F32)
        inv = lax.rsqrt(jnp.mean(v * v, axis=-1, keepdims=True) + NORM_EPS)
        vn_ref[...] = (v * inv * gn_ref[...]).astype(vn_ref.dtype)
        for g in range(SGU_GROUPS):
            cols = slice(g * LANES, (g + 1) * LANES)
            mixed = jnp.dot(wsm_ref[g], vn_ref[:, cols], preferred_element_type=F32) + b_ref[g]
            o_ref[rows, cols] = (u_ref[rows, cols].astype(F32) * mixed).astype(o_ref.dtype)
        return carry

    lax.fori_loop(0, u_ref.shape[0] // SGU_CHUNK, chunk, 0)


def _sgu_mix(z, sgu_gain, w_spatial, b_spatial, *, tm=512):
    s, two_e = z.shape
    e = two_e // 2
    return pl.pallas_call(
        _sgu_kernel,
        grid=(s // tm,),
        in_specs=[
            pl.BlockSpec((tm, e), lambda c: (c, 0)),
            pl.BlockSpec((tm, e), lambda c: (c, 1)),
            pl.BlockSpec((1, e), lambda c: (0, 0)),
            pl.BlockSpec((SGU_GROUPS, SGU_CHUNK, SGU_CHUNK), lambda c: (0, 0, 0)),
            pl.BlockSpec((SGU_GROUPS, SGU_CHUNK, 1), lambda c: (0, 0, 0)),
        ],
        out_specs=pl.BlockSpec((tm, e), lambda c: (c, 0)),
        out_shape=jax.ShapeDtypeStruct((s, e), BF16),
        scratch_shapes=[
            pltpu.VMEM((SGU_GROUPS, SGU_CHUNK, SGU_CHUNK), BF16),
            pltpu.VMEM((SGU_CHUNK, e), BF16),
        ],
        compiler_params=_params(("arbitrary",)),
        name="sgu_mix",
    )(z, z, sgu_gain, w_spatial, b_spatial)


def _moba_kernel(q_ref, k_ref, v_ref, o_ref, kaug_ref, vaug_ref, qaug_ref, kmean_ref, m_ref, acc_ref):
    blk, tile, ktile = MOBA_BLOCK, MOBA_TILE, MOBA_KEY_TILE
    seq = k_ref.shape[0]
    n_blocks = seq // blk
    dn = (((1,), (1,)), ((), ()))

    lane = lax.broadcasted_iota(jnp.int32, (blk, LANES), 1)

    def key_prep(b, carry):
        rows = pl.ds(pl.multiple_of(b * blk, blk), blk)
        kb = k_ref[rows, :]
        kaug_ref[rows, 0:HEAD_DIM] = kb
        kaug_ref[rows, HEAD_DIM:2 * HEAD_DIM] = (lane == b).astype(BF16)
        vaug_ref[rows, 0:HEAD_DIM] = v_ref[rows, :]
        vaug_ref[rows, HEAD_DIM:2 * HEAD_DIM] = jnp.ones((blk, LANES), BF16)
        kmean_ref[pl.ds(b, 1), :] = jnp.mean(kb.astype(F32), axis=0, keepdims=True)
        return carry

    lax.fori_loop(0, n_blocks, key_prep, 0)
    kmean = kmean_ref[...]
    kmean_hi = kmean.astype(BF16)
    kmean_lo = (kmean - kmean_hi.astype(F32)).astype(BF16)

    def query_tile(qt, carry):
        rows = pl.ds(pl.multiple_of(qt * tile, tile), tile)
        q = q_ref[rows, :]

        gate = (lax.dot_general(kmean_hi, q, dn, preferred_element_type=F32)
                + lax.dot_general(kmean_lo, q, dn, preferred_element_type=F32))
        n_iota = lax.broadcasted_iota(jnp.int32, (n_blocks, tile), 0)
        q_blk = qt * (tile // blk) + lax.broadcasted_iota(jnp.int32, (n_blocks, tile), 1) // blk
        past = n_iota < q_blk
        g = jnp.where(past, gate, NEG_BIG)
        sel = jnp.zeros((n_blocks, tile), jnp.bool_)
        for _ in range(MOBA_TOPK):
            mx = jnp.max(g, axis=0, keepdims=True)
            first = jnp.min(jnp.where(g == mx, n_iota, n_blocks), axis=0, keepdims=True)
            pick = n_iota == first
            sel = jnp.logical_or(sel, pick)
            g = jnp.where(pick, -jnp.inf, g)
        allowed = jnp.logical_or(jnp.logical_and(sel, past), n_iota == q_blk)
        mask = jnp.where(allowed, 0.0, NEG_BIG)
        mask = jnp.concatenate([mask, jnp.zeros((LANES - n_blocks, tile), F32)], axis=0)
        qaug_ref[:, 0:HEAD_DIM] = q
        qaug_ref[:, HEAD_DIM:2 * HEAD_DIM] = mask.T.astype(BF16)

        q_groups = [slice(r * blk, (r + 1) * blk) for r in range(tile // blk)]

        row = lax.broadcasted_iota(jnp.int32, (blk, blk), 0)
        col = lax.broadcasted_iota(jnp.int32, (blk, blk), 1)
        seen = [pl.ds(pl.multiple_of(qt * tile, tile), (r + 1) * blk) for r in range(tile // blk)]
        scores = [lax.dot_general(qaug_ref[rr, :], kaug_ref[seen[r], :], dn,
                                  preferred_element_type=F32) for r, rr in enumerate(q_groups)]
        for r, rr in enumerate(q_groups):
            s = scores[r]
            s_own = jnp.where(col <= row, s[:, r * blk:], NEG_BIG)
            s = jnp.concatenate([s[:, :r * blk], s_own], axis=1) if r else s_own
            m0 = jnp.max(s, axis=1, keepdims=True)
            p = jnp.exp2(s - m0)
            m_ref[rr, :] = m0
            acc_ref[rr, :] = jnp.dot(p.astype(BF16), vaug_ref[seen[r], :],
                                     preferred_element_type=F32)

        def past_tile(t, c):
            krows = pl.ds(pl.multiple_of(t * ktile, ktile), ktile)
            scores = [lax.dot_general(qaug_ref[rr, :], kaug_ref[krows, :], dn,
                                      preferred_element_type=F32) for rr in q_groups]
            for rr, sn in zip(q_groups, scores):
                m_prev = m_ref[rr, :]
                m_new = jnp.maximum(m_prev, jnp.max(sn, axis=1, keepdims=True))
                pn = jnp.exp2(sn - m_new)
                acc_ref[rr, :] = jnp.exp2(m_prev - m_new) * acc_ref[rr, :] + jnp.dot(
                    pn.astype(BF16), vaug_ref[krows, :], preferred_element_type=F32)
                m_ref[rr, :] = m_new
            return c

        lax.fori_loop(0, qt * (tile // ktile), past_tile, 0)
        acc = acc_ref[...]
        o_ref[rows, :] = (acc[:, :HEAD_DIM] / acc[:, HEAD_DIM:HEAD_DIM + 1]).astype(o_ref.dtype)
        return carry

    lax.fori_loop(0, seq // tile, query_tile, 0)


def _moba_attention(q, kv):
    s = q.shape[0]
    assert s % MOBA_TILE == 0 and s // MOBA_BLOCK <= LANES
    head = lambda off: pl.BlockSpec((s, HEAD_DIM), lambda h: (0, h + off))
    return pl.pallas_call(
        _moba_kernel,
        grid=(N_HEADS,),
        in_specs=[head(0), head(0), head(N_HEADS)],
        out_specs=head(0),
        out_shape=jax.ShapeDtypeStruct(q.shape, BF16),
        scratch_shapes=[
            pltpu.VMEM((s, 2 * HEAD_DIM), BF16),
            pltpu.VMEM((s, 2 * HEAD_DIM), BF16),
            pltpu.VMEM((MOBA_TILE, 2 * HEAD_DIM), BF16),
            pltpu.VMEM((s // MOBA_BLOCK, HEAD_DIM), F32),
            pltpu.VMEM((MOBA_TILE, 1), F32),
            pltpu.VMEM((MOBA_TILE, 2 * HEAD_DIM), F32),
        ],
        compiler_params=_params(("parallel",)),
        name="moba_attention",
    )(q, kv, kv)


def _scale_rows_kernel(x_ref, inv_ref, g_ref, o_ref):
    o_ref[...] = x_ref[...] * inv_ref[...] * g_ref[...]


def _scale_rows(x, inv, gain, *, tm=256):
    s, d = x.shape
    return pl.pallas_call(
        _scale_rows_kernel,
        grid=(s // tm,),
        in_specs=[
            pl.BlockSpec((tm, d), lambda i: (i, 0)),
            pl.BlockSpec((tm, 1), lambda i: (i, 0)),
            pl.BlockSpec((1, d), lambda i: (0, 0)),
        ],
        out_specs=pl.BlockSpec((tm, d), lambda i: (i, 0)),
        out_shape=jax.ShapeDtypeStruct((s, d), F32),
        compiler_params=_params(("parallel",)),
        name="final_rmsnorm",
    )(x, inv, gain)


def _rope_tables(s):
    inv_freq = 1.0 / np.power(ROPE_THETA, np.arange(0, HEAD_DIM, 2, dtype=np.float64) / HEAD_DIM)
    ang = np.arange(s, dtype=np.float64)[:, None] * inv_freq[None, :]
    cos, sin = np.cos(ang), np.sin(ang)
    return (jnp.asarray(np.concatenate([cos, cos], axis=-1), dtype=F32),
            jnp.asarray(np.concatenate([-sin, sin], axis=-1), dtype=F32))


def _ffn_half_step(x, xb, inv, w13, w2, layer, half, next_gains):
    hmid, w2_bf16 = _swiglu_up(xb, inv, w13, w2, layer, half)
    return _matmul_residual(hmid, w2_bf16, x, next_gains, alpha=MACARON_WEIGHT, tm=512, tn=512)


def kernel(x, ffn_norm, ffn_w13, ffn_w2, mix_norm, a_w_in, a_sgu_norm, a_w_spatial, a_b_spatial,
           a_w_out, kv_norm, w_kv, b_w_q, b_w_o, final_norm):
    batch, s, d = x.shape
    assert batch == 1 and d == N_HEADS * HEAD_DIM
    assert ffn_w13.shape[0] == 2 and a_w_in.shape[0] == 1 and b_w_q.shape[0] == 1

    cos, sin_signed = _rope_tables(s)
    row = lambda g: g.reshape(1, -1)

    h = x[0]
    hb, inv = _norm_prep(h, row(ffn_norm[0, 0]))
    h, hb, inv = _ffn_half_step(h, hb, inv, ffn_w13, ffn_w2, 0, 0, [row(mix_norm[0])])
    z = _gelu_in(hb, inv, a_w_in[0])
    gated = _sgu_mix(z, row(a_sgu_norm[0]), a_w_spatial[0], a_b_spatial[0][:, :, None])
    h, hb, inv = _matmul_residual(gated, a_w_out[0], h, [row(ffn_norm[0, 1])], alpha=1.0,
                                  tm=1024, tn=512)
    h, hb_kv, hb, inv = _ffn_half_step(h, hb, inv, ffn_w13, ffn_w2, 0, 1,
                                       [row(kv_norm), row(ffn_norm[1, 0])])

    kv = _rope_proj(hb_kv, inv, w_kv, cos, sin_signed, rope_cols=d, scale=1.0)

    h, hb, inv = _ffn_half_step(h, hb, inv, ffn_w13, ffn_w2, 1, 0, [row(mix_norm[1])])
    q = _rope_proj(hb, inv, b_w_q[0], cos, sin_signed, rope_cols=d,
                   scale=math.log2(math.e) / math.sqrt(HEAD_DIM))
    attn = _moba_attention(q, kv)
    h, hb, inv = _matmul_residual(attn, b_w_o[0], h, [row(ffn_norm[1, 1])], alpha=1.0,
                                  tm=1024, tn=512)
    h, inv = _ffn_half_step(h, hb, inv, ffn_w13, ffn_w2, 1, 1, [])
    return _scale_rows(h, inv, row(final_norm))[None]
```

```python
import functools
import math

import jax
import jax.numpy as jnp
import numpy as np
from jax import lax
from jax.experimental import pallas as pl
from jax.experimental.pallas import tpu as pltpu

F32 = jnp.float32
BF16 = jnp.bfloat16

NORM_EPS = 1e-6
NEG_BIG = -1e30
MACARON_WEIGHT = 0.5
ROPE_THETA = 10000.0

SGU_CHUNK = 128
SGU_GROUPS = 32
N_HEADS = 32
HEAD_DIM = 128
MOBA_BLOCK = 256
MOBA_TOPK = 3
MOBA_TILE = 2048
MOBA_KEY_TILE = 2048

LANES = 128
FFN_DOWN_TN = 512
VMEM_LIMIT_BYTES = 56 * 1024 * 1024


def _params(semantics):
    return pltpu.CompilerParams(dimension_semantics=semantics, vmem_limit_bytes=VMEM_LIMIT_BYTES)


def _row_tile_spec(tm, d):
    return pl.BlockSpec((tm, d), lambda i, j: (i, 0), pipeline_mode=pl.Buffered(1))


def _mxu_weight(w_ref):
    w = w_ref[...]
    return w if w.dtype == BF16 else w.astype(BF16)


def _norm_prep_kernel(x_ref, g_ref, xb_ref, inv_ref):
    x = x_ref[...]
    inv_ref[...] = lax.rsqrt(jnp.mean(x * x, axis=-1, keepdims=True) + NORM_EPS)
    xb_ref[...] = (x_ref[...] * g_ref[...]).astype(xb_ref.dtype)


def _norm_prep(x, gain, *, tm=256):
    s, d = x.shape
    return pl.pallas_call(
        _norm_prep_kernel,
        grid=(s // tm,),
        in_specs=[pl.BlockSpec((tm, d), lambda i: (i, 0)), pl.BlockSpec((1, d), lambda i: (0, 0))],
        out_specs=[pl.BlockSpec((tm, d), lambda i: (i, 0)), pl.BlockSpec((tm, 1), lambda i: (i, 0))],
        out_shape=[jax.ShapeDtypeStruct((s, d), BF16), jax.ShapeDtypeStruct((s, 1), F32)],
        compiler_params=_params(("parallel",)),
        name="norm_prep",
    )(x, gain)


def _swiglu_up_kernel(xb_ref, inv_ref, wg_ref, wu_ref, w2_ref, o_ref, w2o_ref):
    tn = wg_ref.shape[1]
    w = jnp.concatenate([_mxu_weight(wg_ref), _mxu_weight(wu_ref)], axis=1)
    gu = jnp.dot(xb_ref[...], w, preferred_element_type=F32) * inv_ref[...]
    o_ref[...] = (jax.nn.silu(gu[:, :tn]) * gu[:, tn:]).astype(o_ref.dtype)
    down_tn = w2o_ref.shape[2]
    for c in range(w2o_ref.shape[0]):
        w2o_ref[c] = (MACARON_WEIGHT * w2_ref[:, c * down_tn:(c + 1) * down_tn]).astype(w2o_ref.dtype)


def _swiglu_up(xb, inv, w13, w2, layer, half, *, tm=2048, tn=256, down_tn=FFN_DOWN_TN):
    s, d = xb.shape
    f = w13.shape[-1] // 2
    nj = f // tn
    steps = (s // tm) * nj
    w2_rows = f // steps
    assert w2_rows * steps == f and w2_rows % 16 == 0 and d % down_tn == 0
    return pl.pallas_call(
        _swiglu_up_kernel,
        grid=(s // tm, nj),
        in_specs=[
            _row_tile_spec(tm, d),
            pl.BlockSpec((tm, 1), lambda i, j: (i, 0)),
            pl.BlockSpec((None, None, d, tn), lambda i, j: (layer, half, 0, j)),
            pl.BlockSpec((None, None, d, tn), lambda i, j: (layer, half, 0, j + nj)),
            pl.BlockSpec((None, None, w2_rows, d), lambda i, j: (layer, half, i * nj + j, 0)),
        ],
        out_specs=[
            pl.BlockSpec((tm, tn), lambda i, j: (i, j)),
            pl.BlockSpec((d // down_tn, w2_rows, down_tn), lambda i, j: (0, i * nj + j, 0)),
        ],
        out_shape=[jax.ShapeDtypeStruct((s, f), BF16),
                   jax.ShapeDtypeStruct((d // down_tn, f, down_tn), BF16)],
        compiler_params=_params(("parallel", "parallel")),
        name="swiglu_up",
    )(xb, inv, w13, w13, w2)


def _gelu_in_kernel(xb_ref, inv_ref, w_ref, o_ref):
    z = jnp.dot(xb_ref[...], _mxu_weight(w_ref), preferred_element_type=F32) * inv_ref[...]
    o_ref[...] = (0.5 * z * (1.0 + lax.erf(z * math.sqrt(0.5)))).astype(o_ref.dtype)


def _gelu_in(xb, inv, w, *, tm=2048, tn=512):
    s, d = xb.shape
    n = w.shape[-1]
    return pl.pallas_call(
        _gelu_in_kernel,
        grid=(s // tm, n // tn),
        in_specs=[
            _row_tile_spec(tm, d),
            pl.BlockSpec((tm, 1), lambda i, j: (i, 0)),
            pl.BlockSpec((d, tn), lambda i, j: (0, j)),
        ],
        out_specs=pl.BlockSpec((tm, tn), lambda i, j: (i, j)),
        out_shape=jax.ShapeDtypeStruct((s, n), BF16),
        compiler_params=_params(("parallel", "parallel")),
        name="gelu_in",
    )(xb, inv, w)


def _rope_store(acc, cos, sin_signed, o_ref):
    for h in range(acc.shape[1] // HEAD_DIM):
        cols = slice(h * HEAD_DIM, (h + 1) * HEAD_DIM)
        t = acc[:, cols]
        out = t * cos + pltpu.roll(t, HEAD_DIM // 2, axis=1) * sin_signed
        o_ref[:, cols] = out.astype(o_ref.dtype)


def _rope_proj_kernel(xb_ref, inv_ref, w_ref, cos_ref, sin_ref, o_ref, *, n_rope_tiles, scale):
    j = pl.program_id(1)
    row_scale = inv_ref[...] if scale == 1.0 else inv_ref[...] * scale

    @pl.when(j < n_rope_tiles)
    def _():
        acc = jnp.dot(xb_ref[...], _mxu_weight(w_ref), preferred_element_type=F32) * row_scale
        _rope_store(acc, cos_ref[...], sin_ref[...], o_ref)

    @pl.when(j >= n_rope_tiles)
    def _():
        o_ref[...] = (jnp.dot(xb_ref[...], _mxu_weight(w_ref), preferred_element_type=F32)
                      * inv_ref[...]).astype(o_ref.dtype)


def _rope_proj(xb, inv, w, cos, sin_signed, *, rope_cols, scale, tm=2048, tn=512):
    s, d = xb.shape
    n = w.shape[-1]
    assert rope_cols == n or scale == 1.0
    kern = functools.partial(_rope_proj_kernel, n_rope_tiles=rope_cols // tn, scale=scale)
    return pl.pallas_call(
        kern,
        grid=(s // tm, n // tn),
        in_specs=[
            _row_tile_spec(tm, d),
            pl.BlockSpec((tm, 1), lambda i, j: (i, 0)),
            pl.BlockSpec((d, tn), lambda i, j: (0, j)),
            pl.BlockSpec((tm, HEAD_DIM), lambda i, j: (i, 0)),
            pl.BlockSpec((tm, HEAD_DIM), lambda i, j: (i, 0)),
        ],
        out_specs=pl.BlockSpec((tm, tn), lambda i, j: (i, j)),
        out_shape=jax.ShapeDtypeStruct((s, n), BF16),
        compiler_params=_params(("parallel", "parallel")),
        name="rope_proj",
    )(xb, inv, w, cos, sin_signed)


def _matmul_residual_kernel(a_ref, w_ref, r_ref, *rest, n_gains, d):
    gain_refs = rest[:n_gains]
    o_ref = rest[n_gains]
    xb_refs = rest[n_gains + 1:2 * n_gains + 1]
    inv_ref = rest[2 * n_gains + 1]
    j = pl.program_id(1)

    y = r_ref[...] + jnp.dot(a_ref[...], _mxu_weight(w_ref), preferred_element_type=F32)
    o_ref[...] = y
    for g_ref, xb_ref in zip(gain_refs, xb_refs):
        xb_ref[...] = (y * g_ref[...]).astype(xb_ref.dtype)

    ss = jnp.sum(y * y, axis=-1, keepdims=True)

    @pl.when(j == 0)
    def _():
        inv_ref[...] = ss

    @pl.when(j > 0)
    def _():
        inv_ref[...] += ss

    @pl.when(j == pl.num_programs(1) - 1)
    def _():
        inv_ref[...] = lax.rsqrt(inv_ref[...] * (1.0 / d) + NORM_EPS)


def _matmul_residual(a, w, res, gains, *, tm, tn=None):
    s, k = a.shape
    if w.ndim == 3:
        assert tn in (None, w.shape[2])
        tn = w.shape[2]
        n = w.shape[0] * tn
        w_spec = pl.BlockSpec((None, k, tn), lambda i, j: (j, 0, 0))
    else:
        n = w.shape[1]
        w_spec = pl.BlockSpec((k, tn), lambda i, j: (0, j))
    n_gains = len(gains)
    tile = lambda dtype: (pl.BlockSpec((tm, tn), lambda i, j: (i, j)), jax.ShapeDtypeStruct((s, n), dtype))
    outs = [tile(F32)] + [tile(BF16)] * n_gains
    outs.append((pl.BlockSpec((tm, 1), lambda i, j: (i, 0)), jax.ShapeDtypeStruct((s, 1), F32)))
    return pl.pallas_call(
        functools.partial(_matmul_residual_kernel, n_gains=n_gains, d=n),
        grid=(s // tm, n // tn),
        in_specs=[
            pl.BlockSpec((tm, k), lambda i, j: (i, 0)),
            w_spec,
            pl.BlockSpec((tm, tn), lambda i, j: (i, j)),
        ] + [pl.BlockSpec((1, tn), lambda i, j: (0, j))] * n_gains,
        out_specs=[spec for spec, _ in outs],
        out_shape=[shape for _, shape in outs],
        compiler_params=_params(("parallel", "arbitrary")),
        name="matmul_residual",
    )(a, w, res, *gains)


def _sgu_kernel(u_ref, v_ref, gn_ref, ws_ref, b_ref, o_ref, wsm_ref, vn_ref):
    @pl.when(pl.program_id(0) == 0)
    def _():
        t = lax.broadcasted_iota(jnp.int32, (SGU_CHUNK, SGU_CHUNK), 0)
        s = lax.broadcasted_iota(jnp.int32, (SGU_CHUNK, SGU_CHUNK), 1)
        for g in range(SGU_GROUPS):
            wsm_ref[g] = jnp.where(s <= t, ws_ref[g], 0.0).astype(wsm_ref.dtype)

    def chunk(c, carry):
        rows = pl.ds(pl.multiple_of(c * SGU_CHUNK, SGU_CHUNK), SGU_CHUNK)
        v = v_ref[rows, :].astype(F32)
        inv = lax.rsqrt(jnp.mean(v * v, axis=-1, keepdims=True) + NORM_EPS)
        vn_ref[...] = (v * inv * gn_ref[...]).astype(vn_ref.dtype)
        for g in range(SGU_GROUPS):
            cols = slice(g * LANES, (g + 1) * LANES)
            mixed = jnp.dot(wsm_ref[g], vn_ref[:, cols], preferred_element_type=F32) + b_ref[g]
            o_ref[rows, cols] = (u_ref[rows, cols].astype(F32) * mixed).astype(o_ref.dtype)
        return carry

    lax.fori_loop(0, u_ref.shape[0] // SGU_CHUNK, chunk, 0)


def _sgu_mix(z, sgu_gain, w_spatial, b_spatial, *, tm=512):
    s, two_e = z.shape
    e = two_e // 2
    return pl.pallas_call(
        _sgu_kernel,
        grid=(s // tm,),
        in_specs=[
            pl.BlockSpec((tm, e), lambda c: (c, 0)),
            pl.BlockSpec((tm, e), lambda c: (c, 1)),
            pl.BlockSpec((1, e), lambda c: (0, 0)),
            pl.BlockSpec((SGU_GROUPS, SGU_CHUNK, SGU_CHUNK), lambda c: (0, 0, 0)),
            pl.BlockSpec((SGU_GROUPS, SGU_CHUNK, 1), lambda c: (0, 0, 0)),
        ],
        out_specs=pl.BlockSpec((tm, e), lambda c: (c, 0)),
        out_shape=jax.ShapeDtypeStruct((s, e), BF16),
        scratch_shapes=[
            pltpu.VMEM((SGU_GROUPS, SGU_CHUNK, SGU_CHUNK), BF16),
            pltpu.VMEM((SGU_CHUNK, e), BF16),
        ],
        compiler_params=_params(("arbitrary",)),
        name="sgu_mix",
    )(z, z, sgu_gain, w_spatial, b_spatial)


def _moba_kernel(q_ref, k_ref, v_ref, o_ref, kaug_ref, vaug_ref, qaug_ref, kmean_ref, m_ref, acc_ref):
    blk, tile, ktile = MOBA_BLOCK, MOBA_TILE, MOBA_KEY_TILE
    seq = k_ref.shape[0]
    n_blocks = seq // blk
    dn = (((1,), (1,)), ((), ()))

    lane = lax.broadcasted_iota(jnp.int32, (blk, LANES), 1)

    def key_prep(b, carry):
        rows = pl.ds(pl.multiple_of(b * blk, blk), blk)
        kb = k_ref[rows, :]
        kaug_ref[rows, 0:HEAD_DIM] = kb
        kaug_ref[rows, HEAD_DIM:2 * HEAD_DIM] = (lane == b).astype(BF16)
        vaug_ref[rows, 0:HEAD_DIM] = v_ref[rows, :]
        vaug_ref[rows, HEAD_DIM:2 * HEAD_DIM] = jnp.ones((blk, LANES), BF16)
        kmean_ref[pl.ds(b, 1), :] = jnp.mean(kb.astype(F32), axis=0, keepdims=True)
        return carry

    lax.fori_loop(0, n_blocks, key_prep, 0)
    kmean = kmean_ref[...]
    kmean_hi = kmean.astype(BF16)
    kmean_lo = (kmean - kmean_hi.astype(F32)).astype(BF16)

    def query_tile(qt, carry):
        rows = pl.ds(pl.multiple_of(qt * tile, tile), tile)
        q = q_ref[rows, :]

        gate = (lax.dot_general(kmean_hi, q, dn, preferred_element_type=F32)
                + lax.dot_general(kmean_lo, q, dn, preferred_element_type=F32))
        n_iota = lax.broadcasted_iota(jnp.int32, (n_blocks, tile), 0)
        q_blk = qt * (tile // blk) + lax.broadcasted_iota(jnp.int32, (n_blocks, tile), 1) // blk
        past = n_iota < q_blk
        g = jnp.where(past, gate, NEG_BIG)
        sel = jnp.zeros((n_blocks, tile), jnp.bool_)
        for _ in range(MOBA_TOPK):
            mx = jnp.max(g, axis=0, keepdims=True)
            first = jnp.min(jnp.where(g == mx, n_iota, n_blocks), axis=0, keepdims=True)
            pick = n_iota == first
            sel = jnp.logical_or(sel, pick)
            g = jnp.where(pick, -jnp.inf, g)
        allowed = jnp.logical_or(jnp.logical_and(sel, past), n_iota == q_blk)
        mask = jnp.where(allowed, 0.0, NEG_BIG)
        mask = jnp.concatenate([mask, jnp.zeros((LANES - n_blocks, tile), F32)], axis=0)
        qaug_ref[:, 0:HEAD_DIM] = q
        qaug_ref[:, HEAD_DIM:2 * HEAD_DIM] = mask.T.astype(BF16)

        q_groups = [slice(r * blk, (r + 1) * blk) for r in range(tile // blk)]

        row = lax.broadcasted_iota(jnp.int32, (blk, blk), 0)
        col = lax.broadcasted_iota(jnp.int32, (blk, blk), 1)
        seen = [pl.ds(pl.multiple_of(qt * tile, tile), (r + 1) * blk) for r in range(tile // blk)]
        scores = [lax.dot_general(qaug_ref[rr, :], kaug_ref[seen[r], :], dn,
                                  preferred_element_type=F32) for r, rr in enumerate(q_groups)]
        for r, rr in enumerate(q_groups):
            s = scores[r]
            s_own = jnp.where(col <= row, s[:, r * blk:], NEG_BIG)
            s = jnp.concatenate([s[:, :r * blk], s_own], axis=1) if r else s_own
            m0 = jnp.max(s, axis=1, keepdims=True)
            p = jnp.exp2(s - m0)
            m_ref[rr, :] = m0
            acc_ref[rr, :] = jnp.dot(p.astype(BF16), vaug_ref[seen[r], :],
                                     preferred_element_type=F32)

        def past_tile(t, c):
            krows = pl.ds(pl.multiple_of(t * ktile, ktile), ktile)
            scores = [lax.dot_general(qaug_ref[rr, :], kaug_ref[krows, :], dn,
                                      preferred_element_type=F32) for rr in q_groups]
            for rr, sn in zip(q_groups, scores):
                m_prev = m_ref[rr, :]
                m_new = jnp.maximum(m_prev, jnp.max(sn, axis=1, keepdims=True))
                pn = jnp.exp2(sn - m_new)
                acc_ref[rr, :] = jnp.exp2(m_prev - m_new) * acc_ref[rr, :] + jnp.dot(
                    pn.astype(BF16), vaug_ref[krows, :], preferred_element_type=F32)
                m_ref[rr, :] = m_new
            return c

        lax.fori_loop(0, qt * (tile // ktile), past_tile, 0)
        acc = acc_ref[...]
        o_ref[rows, :] = (acc[:, :HEAD_DIM] / acc[:, HEAD_DIM:HEAD_DIM + 1]).astype(o_ref.dtype)
        return carry

    lax.fori_loop(0, seq // tile, query_tile, 0)


def _moba_attention(q, kv):
    s = q.shape[0]
    assert s % MOBA_TILE == 0 and s // MOBA_BLOCK <= LANES
    head = lambda off: pl.BlockSpec((s, HEAD_DIM), lambda h: (0, h + off))
    return pl.pallas_call(
        _moba_kernel,
        grid=(N_HEADS,),
        in_specs=[head(0), head(0), head(N_HEADS)],
        out_specs=head(0),
        out_shape=jax.ShapeDtypeStruct(q.shape, BF16),
        scratch_shapes=[
            pltpu.VMEM((s, 2 * HEAD_DIM), BF16),
            pltpu.VMEM((s, 2 * HEAD_DIM), BF16),
            pltpu.VMEM((MOBA_TILE, 2 * HEAD_DIM), BF16),
            pltpu.VMEM((s // MOBA_BLOCK, HEAD_DIM), F32),
            pltpu.VMEM((MOBA_TILE, 1), F32),
            pltpu.VMEM((MOBA_TILE, 2 * HEAD_DIM), F32),
        ],
        compiler_params=_params(("parallel",)),
        name="moba_attention",
    )(q, kv, kv)


def _scale_rows_kernel(x_ref, inv_ref, g_ref, o_ref):
    o_ref[...] = x_ref[...] * inv_ref[...] * g_ref[...]


def _scale_rows(x, inv, gain, *, tm=256):
    s, d = x.shape
    return pl.pallas_call(
        _scale_rows_kernel,
        grid=(s // tm,),
        in_specs=[
            pl.BlockSpec((tm, d), lambda i: (i, 0)),
            pl.BlockSpec((tm, 1), lambda i: (i, 0)),
            pl.BlockSpec((1, d), lambda i: (0, 0)),
        ],
        out_specs=pl.BlockSpec((tm, d), lambda i: (i, 0)),
        out_shape=jax.ShapeDtypeStruct((s, d), F32),
        compiler_params=_params(("parallel",)),
        name="final_rmsnorm",
    )(x, inv, gain)


def _rope_tables(s):
    inv_freq = 1.0 / np.power(ROPE_THETA, np.arange(0, HEAD_DIM, 2, dtype=np.float64) / HEAD_DIM)
    ang = np.arange(s, dtype=np.float64)[:, None] * inv_freq[None, :]
    cos, sin = np.cos(ang), np.sin(ang)
    return (jnp.asarray(np.concatenate([cos, cos], axis=-1), dtype=F32),
            jnp.asarray(np.concatenate([-sin, sin], axis=-1), dtype=F32))


def _ffn_half_step(x, xb, inv, w13, w2, layer, half, next_gains):
    hmid, w2_bf16 = _swiglu_up(xb, inv, w13, w2, layer, half)
    return _matmul_residual(hmid, w2_bf16, x, next_gains, tm=512)


def kernel(x, ffn_norm, ffn_w13, ffn_w2, mix_norm, a_w_in, a_sgu_norm, a_w_spatial, a_b_spatial,
           a_w_out, kv_norm, w_kv, b_w_q, b_w_o, final_norm):
    batch, s, d = x.shape
    assert batch == 1 and d == N_HEADS * HEAD_DIM
    assert ffn_w13.shape[0] == 2 and a_w_in.shape[0] == 1 and b_w_q.shape[0] == 1

    cos, sin_signed = _rope_tables(s)
    row = lambda g: g.reshape(1, -1)

    h = x[0]
    hb, inv = _norm_prep(h, row(ffn_norm[0, 0]))
    h, hb, inv = _ffn_half_step(h, hb, inv, ffn_w13, ffn_w2, 0, 0, [row(mix_norm[0])])
    z = _gelu_in(hb, inv, a_w_in[0])
    gated = _sgu_mix(z, row(a_sgu_norm[0]), a_w_spatial[0], a_b_spatial[0][:, :, None])
    h, hb, inv = _matmul_residual(gated, a_w_out[0], h, [row(ffn_norm[0, 1])], tm=1024, tn=512)
    h, hb_kv, hb, inv = _ffn_half_step(h, hb, inv, ffn_w13, ffn_w2, 0, 1,
                                       [row(kv_norm), row(ffn_norm[1, 0])])

    kv = _rope_proj(hb_kv, inv, w_kv, cos, sin_signed, rope_cols=d, scale=1.0)

    h, hb, inv = _ffn_half_step(h, hb, inv, ffn_w13, ffn_w2, 1, 0, [row(mix_norm[1])])
    q = _rope_proj(hb, inv, b_w_q[0], cos, sin_signed, rope_cols=d,
                   scale=math.log2(math.e) / math.sqrt(HEAD_DIM))
    attn = _moba_attention(q, kv)
    h, hb, inv = _matmul_residual(attn, b_w_o[0], h, [row(ffn_norm[1, 1])], tm=1024, tn=512)
    h, inv = _ffn_half_step(h, hb, inv, ffn_w13, ffn_w2, 1, 1, [])
    return _scale_rows(h, inv, row(final_norm))[None]
```

```python
import functools
import math

import jax
import jax.numpy as jnp
import numpy as np
from jax import lax
from jax.experimental import pallas as pl
from jax.experimental.pallas import tpu as pltpu

F32 = jnp.float32
BF16 = jnp.bfloat16

NORM_EPS = 1e-6
NEG_BIG = -1e30
MACARON_WEIGHT = 0.5
ROPE_THETA = 10000.0

SGU_CHUNK = 128
SGU_GROUPS = 32
N_HEADS = 32
HEAD_DIM = 128
MOBA_BLOCK = 256
MOBA_TOPK = 3
MOBA_TILE = 2048
MOBA_KEY_TILE = 2048

LANES = 128
VMEM_LIMIT_BYTES = 56 * 1024 * 1024

NORMED_TM = 2048
SWIGLU_TN = 256
PROJ_TN = 512
DOWN_TM, DOWN_TN = 512, 512
OUT_PROJ_TM, OUT_PROJ_TN = 1024, 512
ROWWISE_TM = 256
SGU_TM = 512


def _params(semantics):
    return pltpu.CompilerParams(dimension_semantics=semantics, vmem_limit_bytes=VMEM_LIMIT_BYTES)


def _row_tile_spec(tm, d):
    return pl.BlockSpec((tm, d), lambda i, j: (i, 0), pipeline_mode=pl.Buffered(1))


def _mxu_weight(w_ref):
    w = w_ref[...]
    return w if w.dtype == BF16 else w.astype(BF16)


def _norm_prep_kernel(x_ref, g_ref, xb_ref, inv_ref):
    x = x_ref[...]
    inv_ref[...] = lax.rsqrt(jnp.mean(x * x, axis=-1, keepdims=True) + NORM_EPS)
    xb_ref[...] = (x_ref[...] * g_ref[...]).astype(xb_ref.dtype)


def _norm_prep(x, gain, *, tm=ROWWISE_TM):
    s, d = x.shape
    return pl.pallas_call(
        _norm_prep_kernel,
        grid=(s // tm,),
        in_specs=[pl.BlockSpec((tm, d), lambda i: (i, 0)), pl.BlockSpec((1, d), lambda i: (0, 0))],
        out_specs=[pl.BlockSpec((tm, d), lambda i: (i, 0)), pl.BlockSpec((tm, 1), lambda i: (i, 0))],
        out_shape=[jax.ShapeDtypeStruct((s, d), BF16), jax.ShapeDtypeStruct((s, 1), F32)],
        compiler_params=_params(("parallel",)),
        name="norm_prep",
    )(x, gain)


def _swiglu_up_kernel(xb_ref, inv_ref, wg_ref, wu_ref, w2_ref, o_ref, w2o_ref):
    tn = wg_ref.shape[1]
    w = jnp.concatenate([_mxu_weight(wg_ref), _mxu_weight(wu_ref)], axis=1)
    gu = jnp.dot(xb_ref[...], w, preferred_element_type=F32)
    inv = inv_ref[...]
    o_ref[...] = (jax.nn.silu(gu[:, :tn] * inv) * (gu[:, tn:] * (MACARON_WEIGHT * inv))).astype(o_ref.dtype)
    w2o_ref[...] = w2_ref[...].astype(w2o_ref.dtype)


def _swiglu_up(xb, inv, w13, w2, layer, half, *, tm=NORMED_TM, tn=SWIGLU_TN):
    s, d = xb.shape
    f = w13.shape[-1] // 2
    nj = f // tn
    steps = (s // tm) * nj
    w2_rows = f // steps
    assert w2_rows * steps == f and w2_rows % 16 == 0
    return pl.pallas_call(
        _swiglu_up_kernel,
        grid=(s // tm, nj),
        in_specs=[
            _row_tile_spec(tm, d),
            pl.BlockSpec((tm, 1), lambda i, j: (i, 0)),
            pl.BlockSpec((None, None, d, tn), lambda i, j: (layer, half, 0, j)),
            pl.BlockSpec((None, None, d, tn), lambda i, j: (layer, half, 0, j + nj)),
            pl.BlockSpec((None, None, w2_rows, d), lambda i, j: (layer, half, i * nj + j, 0)),
        ],
        out_specs=[
            pl.BlockSpec((tm, tn), lambda i, j: (i, j)),
            pl.BlockSpec((w2_rows, d), lambda i, j: (i * nj + j, 0)),
        ],
        out_shape=[jax.ShapeDtypeStruct((s, f), BF16), jax.ShapeDtypeStruct((f, d), BF16)],
        compiler_params=_params(("parallel", "parallel")),
        name="swiglu_up",
    )(xb, inv, w13, w13, w2)


def _gelu_in_kernel(xb_ref, inv_ref, w_ref, o_ref):
    z = jnp.dot(xb_ref[...], _mxu_weight(w_ref), preferred_element_type=F32) * inv_ref[...]
    o_ref[...] = (0.5 * z * (1.0 + lax.erf(z * math.sqrt(0.5)))).astype(o_ref.dtype)


def _gelu_in(xb, inv, w, *, tm=NORMED_TM, tn=PROJ_TN):
    s, d = xb.shape
    n = w.shape[-1]
    return pl.pallas_call(
        _gelu_in_kernel,
        grid=(s // tm, n // tn),
        in_specs=[
            _row_tile_spec(tm, d),
            pl.BlockSpec((tm, 1), lambda i, j: (i, 0)),
            pl.BlockSpec((d, tn), lambda i, j: (0, j)),
        ],
        out_specs=pl.BlockSpec((tm, tn), lambda i, j: (i, j)),
        out_shape=jax.ShapeDtypeStruct((s, n), BF16),
        compiler_params=_params(("parallel", "parallel")),
        name="gelu_in",
    )(xb, inv, w)


def _rope_store(acc, cos, sin_signed, o_ref):
    for h in range(acc.shape[1] // HEAD_DIM):
        cols = slice(h * HEAD_DIM, (h + 1) * HEAD_DIM)
        t = acc[:, cols]
        out = t * cos + pltpu.roll(t, HEAD_DIM // 2, axis=1) * sin_signed
        o_ref[:, cols] = out.astype(o_ref.dtype)


def _rope_proj_kernel(xb_ref, inv_ref, w_ref, cos_ref, sin_ref, o_ref, *, n_rope_tiles, scale):
    j = pl.program_id(1)
    row_scale = inv_ref[...] if scale == 1.0 else inv_ref[...] * scale

    @pl.when(j < n_rope_tiles)
    def _():
        acc = jnp.dot(xb_ref[...], _mxu_weight(w_ref), preferred_element_type=F32) * row_scale
        _rope_store(acc, cos_ref[...], sin_ref[...], o_ref)

    @pl.when(j >= n_rope_tiles)
    def _():
        o_ref[...] = (jnp.dot(xb_ref[...], _mxu_weight(w_ref), preferred_element_type=F32)
                      * inv_ref[...]).astype(o_ref.dtype)


def _rope_proj(xb, inv, w, cos, sin_signed, *, rope_cols, scale, tm=NORMED_TM, tn=PROJ_TN):
    s, d = xb.shape
    n = w.shape[-1]
    assert rope_cols == n or scale == 1.0
    kern = functools.partial(_rope_proj_kernel, n_rope_tiles=rope_cols // tn, scale=scale)
    return pl.pallas_call(
        kern,
        grid=(s // tm, n // tn),
        in_specs=[
            _row_tile_spec(tm, d),
            pl.BlockSpec((tm, 1), lambda i, j: (i, 0)),
            pl.BlockSpec((d, tn), lambda i, j: (0, j)),
            pl.BlockSpec((tm, HEAD_DIM), lambda i, j: (i, 0)),
            pl.BlockSpec((tm, HEAD_DIM), lambda i, j: (i, 0)),
        ],
        out_specs=pl.BlockSpec((tm, tn), lambda i, j: (i, j)),
        out_shape=jax.ShapeDtypeStruct((s, n), BF16),
        compiler_params=_params(("parallel", "parallel")),
        name="rope_proj",
    )(xb, inv, w, cos, sin_signed)


def _matmul_residual_kernel(a_ref, w_ref, r_ref, *rest, n_gains, d):
    gain_refs = rest[:n_gains]
    o_ref = rest[n_gains]
    xb_refs = rest[n_gains + 1:2 * n_gains + 1]
    inv_ref = rest[2 * n_gains + 1]
    j = pl.program_id(1)

    y = r_ref[...] + jnp.dot(a_ref[...], _mxu_weight(w_ref), preferred_element_type=F32)
    o_ref[...] = y
    for g_ref, xb_ref in zip(gain_refs, xb_refs):
        xb_ref[...] = (y * g_ref[...]).astype(xb_ref.dtype)

    ss = jnp.sum(y * y, axis=-1, keepdims=True)

    @pl.when(j == 0)
    def _():
        inv_ref[...] = ss

    @pl.when(j > 0)
    def _():
        inv_ref[...] += ss

    @pl.when(j == pl.num_programs(1) - 1)
    def _():
        inv_ref[...] = lax.rsqrt(inv_ref[...] * (1.0 / d) + NORM_EPS)


def _matmul_residual(a, w, res, gains, *, tm, tn):
    s, k = a.shape
    n = w.shape[1]
    n_gains = len(gains)
    tile = lambda dtype: (pl.BlockSpec((tm, tn), lambda i, j: (i, j)), jax.ShapeDtypeStruct((s, n), dtype))
    outs = [tile(F32)] + [tile(BF16)] * n_gains
    outs.append((pl.BlockSpec((tm, 1), lambda i, j: (i, 0)), jax.ShapeDtypeStruct((s, 1), F32)))
    return pl.pallas_call(
        functools.partial(_matmul_residual_kernel, n_gains=n_gains, d=n),
        grid=(s // tm, n // tn),
        in_specs=[
            pl.BlockSpec((tm, k), lambda i, j: (i, 0)),
            pl.BlockSpec((k, tn), lambda i, j: (0, j)),
            pl.BlockSpec((tm, tn), lambda i, j: (i, j)),
        ] + [pl.BlockSpec((1, tn), lambda i, j: (0, j))] * n_gains,
        out_specs=[spec for spec, _ in outs],
        out_shape=[shape for _, shape in outs],
        compiler_params=_params(("parallel", "arbitrary")),
        name="matmul_residual",
    )(a, w, res, *gains)


def _sgu_kernel(u_ref, v_ref, gn_ref, ws_ref, b_ref, o_ref, wsm_ref, vn_ref):
    @pl.when(pl.program_id(0) == 0)
    def _():
        t = lax.broadcasted_iota(jnp.int32, (SGU_CHUNK, SGU_CHUNK), 0)
        s = lax.broadcasted_iota(jnp.int32, (SGU_CHUNK, SGU_CHUNK), 1)
        for g in range(SGU_GROUPS):
            wsm_ref[g] = jnp.where(s <= t, ws_ref[g], 0.0).astype(wsm_ref.dtype)

    def chunk(c, carry):
        rows = pl.ds(pl.multiple_of(c * SGU_CHUNK, SGU_CHUNK), SGU_CHUNK)
        v = v_ref[rows, :].astype(F32)
        inv = lax.rsqrt(jnp.mean(v * v, axis=-1, keepdims=True) + NORM_EPS)
        vn_ref[...] = (v * inv * gn_ref[...]).astype(vn_ref.dtype)
        for g in range(SGU_GROUPS):
            cols = slice(g * LANES, (g + 1) * LANES)
            mixed = jnp.dot(wsm_ref[g], vn_ref[:, cols], preferred_element_type=F32) + b_ref[g]
            o_ref[rows, cols] = (u_ref[rows, cols].astype(F32) * mixed).astype(o_ref.dtype)
        return carry

    lax.fori_loop(0, u_ref.shape[0] // SGU_CHUNK, chunk, 0)


def _sgu_mix(z, sgu_gain, w_spatial, b_spatial, *, tm=SGU_TM):
    s, two_e = z.shape
    e = two_e // 2
    return pl.pallas_call(
        _sgu_kernel,
        grid=(s // tm,),
        in_specs=[
            pl.BlockSpec((tm, e), lambda c: (c, 0)),
            pl.BlockSpec((tm, e), lambda c: (c, 1)),
            pl.BlockSpec((1, e), lambda c: (0, 0)),
            pl.BlockSpec((SGU_GROUPS, SGU_CHUNK, SGU_CHUNK), lambda c: (0, 0, 0)),
            pl.BlockSpec((SGU_GROUPS, SGU_CHUNK, 1), lambda c: (0, 0, 0)),
        ],
        out_specs=pl.BlockSpec((tm, e), lambda c: (c, 0)),
        out_shape=jax.ShapeDtypeStruct((s, e), BF16),
        scratch_shapes=[
            pltpu.VMEM((SGU_GROUPS, SGU_CHUNK, SGU_CHUNK), BF16),
            pltpu.VMEM((SGU_CHUNK, e), BF16),
        ],
        compiler_params=_params(("arbitrary",)),
        name="sgu_mix",
    )(z, z, sgu_gain, w_spatial, b_spatial)


def _moba_kernel(q_ref, k_ref, v_ref, o_ref, kaug_ref, vaug_ref, qaug_ref, kmean_ref, m_ref, acc_ref):
    blk, tile, ktile = MOBA_BLOCK, MOBA_TILE, MOBA_KEY_TILE
    seq = k_ref.shape[0]
    n_blocks = seq // blk
    dn = (((1,), (1,)), ((), ()))

    lane = lax.broadcasted_iota(jnp.int32, (blk, LANES), 1)

    def key_prep(b, carry):
        rows = pl.ds(pl.multiple_of(b * blk, blk), blk)
        kb = k_ref[rows, :]
        kaug_ref[rows, 0:HEAD_DIM] = kb
        kaug_ref[rows, HEAD_DIM:2 * HEAD_DIM] = (lane == b).astype(BF16)
        vaug_ref[rows, 0:HEAD_DIM] = v_ref[rows, :]
        vaug_ref[rows, HEAD_DIM:2 * HEAD_DIM] = jnp.ones((blk, LANES), BF16)
        kmean_ref[pl.ds(b, 1), :] = jnp.mean(kb.astype(F32), axis=0, keepdims=True)
        return carry

    lax.fori_loop(0, n_blocks, key_prep, 0)
    kmean = kmean_ref[...]
    kmean_hi = kmean.astype(BF16)
    kmean_lo = (kmean - kmean_hi.astype(F32)).astype(BF16)

    def query_tile(qt, carry):
        rows = pl.ds(pl.multiple_of(qt * tile, tile), tile)
        q = q_ref[rows, :]

        gate = (lax.dot_general(kmean_hi, q, dn, preferred_element_type=F32)
                + lax.dot_general(kmean_lo, q, dn, preferred_element_type=F32))
        n_iota = lax.broadcasted_iota(jnp.int32, (n_blocks, tile), 0)
        q_blk = qt * (tile // blk) + lax.broadcasted_iota(jnp.int32, (n_blocks, tile), 1) // blk
        past = n_iota < q_blk
        g = jnp.where(past, gate, NEG_BIG)
        sel = jnp.zeros((n_blocks, tile), jnp.bool_)
        for _ in range(MOBA_TOPK):
            mx = jnp.max(g, axis=0, keepdims=True)
            first = jnp.min(jnp.where(g == mx, n_iota, n_blocks), axis=0, keepdims=True)
            pick = n_iota == first
            sel = jnp.logical_or(sel, pick)
            g = jnp.where(pick, -jnp.inf, g)
        allowed = jnp.logical_or(jnp.logical_and(sel, past), n_iota == q_blk)
        mask = jnp.where(allowed, 0.0, NEG_BIG)
        mask = jnp.concatenate([mask, jnp.zeros((LANES - n_blocks, tile), F32)], axis=0)
        qaug_ref[:, 0:HEAD_DIM] = q
        qaug_ref[:, HEAD_DIM:2 * HEAD_DIM] = mask.T.astype(BF16)

        q_groups = [slice(r * blk, (r + 1) * blk) for r in range(tile // blk)]

        row = lax.broadcasted_iota(jnp.int32, (blk, blk), 0)
        col = lax.broadcasted_iota(jnp.int32, (blk, blk), 1)
        seen = [pl.ds(pl.multiple_of(qt * tile, tile), (r + 1) * blk) for r in range(tile // blk)]
        scores = [lax.dot_general(qaug_ref[rr, :], kaug_ref[seen[r], :], dn,
                                  preferred_element_type=F32) for r, rr in enumerate(q_groups)]
        for r, rr in enumerate(q_groups):
            s = scores[r]
            s_own = jnp.where(col <= row, s[:, r * blk:], NEG_BIG)
            s = jnp.concatenate([s[:, :r * blk], s_own], axis=1) if r else s_own
            m0 = jnp.max(s, axis=1, keepdims=True)
            p = jnp.exp2(s - m0)
            m_ref[rr, :] = m0
            acc_ref[rr, :] = jnp.dot(p.astype(BF16), vaug_ref[seen[r], :],
                                     preferred_element_type=F32)

        def past_tile(t, c):
            krows = pl.ds(pl.multiple_of(t * ktile, ktile), ktile)
            scores = [lax.dot_general(qaug_ref[rr, :], kaug_ref[krows, :], dn,
                                      preferred_element_type=F32) for rr in q_groups]
            for rr, sn in zip(q_groups, scores):
                m_prev = m_ref[rr, :]
                m_new = jnp.maximum(m_prev, jnp.max(sn, axis=1, keepdims=True))
                pn = jnp.exp2(sn - m_new)
                acc_ref[rr, :] = jnp.exp2(m_prev - m_new) * acc_ref[rr, :] + jnp.dot(
                    pn.astype(BF16), vaug_ref[krows, :], preferred_element_type=F32)
                m_ref[rr, :] = m_new
            return c

        lax.fori_loop(0, qt * (tile // ktile), past_tile, 0)
        acc = acc_ref[...]
        o_ref[rows, :] = (acc[:, :HEAD_DIM] / acc[:, HEAD_DIM:HEAD_DIM + 1]).astype(o_ref.dtype)
        return carry

    lax.fori_loop(0, seq // tile, query_tile, 0)


def _moba_attention(q, kv):
    s = q.shape[0]
    assert s % MOBA_TILE == 0 and s // MOBA_BLOCK <= LANES
    head = lambda off: pl.BlockSpec((s, HEAD_DIM), lambda h: (0, h + off))
    return pl.pallas_call(
        _moba_kernel,
        grid=(N_HEADS,),
        in_specs=[head(0), head(0), head(N_HEADS)],
        out_specs=head(0),
        out_shape=jax.ShapeDtypeStruct(q.shape, BF16),
        scratch_shapes=[
            pltpu.VMEM((s, 2 * HEAD_DIM), BF16),
            pltpu.VMEM((s, 2 * HEAD_DIM), BF16),
            pltpu.VMEM((MOBA_TILE, 2 * HEAD_DIM), BF16),
            pltpu.VMEM((s // MOBA_BLOCK, HEAD_DIM), F32),
            pltpu.VMEM((MOBA_TILE, 1), F32),
            pltpu.VMEM((MOBA_TILE, 2 * HEAD_DIM), F32),
        ],
        compiler_params=_params(("parallel",)),
        name="moba_attention",
    )(q, kv, kv)


def _scale_rows_kernel(x_ref, inv_ref, g_ref, o_ref):
    o_ref[...] = x_ref[...] * inv_ref[...] * g_ref[...]


def _scale_rows(x, inv, gain, *, tm=ROWWISE_TM):
    s, d = x.shape
    return pl.pallas_call(
        _scale_rows_kernel,
        grid=(s // tm,),
        in_specs=[
            pl.BlockSpec((tm, d), lambda i: (i, 0)),
            pl.BlockSpec((tm, 1), lambda i: (i, 0)),
            pl.BlockSpec((1, d), lambda i: (0, 0)),
        ],
        out_specs=pl.BlockSpec((tm, d), lambda i: (i, 0)),
        out_shape=jax.ShapeDtypeStruct((s, d), F32),
        compiler_params=_params(("parallel",)),
        name="final_rmsnorm",
    )(x, inv, gain)


def _rope_tables(s):
    inv_freq = 1.0 / np.power(ROPE_THETA, np.arange(0, HEAD_DIM, 2, dtype=np.float64) / HEAD_DIM)
    ang = np.arange(s, dtype=np.float64)[:, None] * inv_freq[None, :]
    cos, sin = np.cos(ang), np.sin(ang)
    return (jnp.asarray(np.concatenate([cos, cos], axis=-1), dtype=F32),
            jnp.asarray(np.concatenate([-sin, sin], axis=-1), dtype=F32))


def _ffn_half_step(x, xb, inv, w13, w2, layer, half, next_gains):
    hmid, w2_bf16 = _swiglu_up(xb, inv, w13, w2, layer, half)
    return _matmul_residual(hmid, w2_bf16, x, next_gains, tm=DOWN_TM, tn=DOWN_TN)


def kernel(x, ffn_norm, ffn_w13, ffn_w2, mix_norm, a_w_in, a_sgu_norm, a_w_spatial, a_b_spatial,
           a_w_out, kv_norm, w_kv, b_w_q, b_w_o, final_norm):
    batch, s, d = x.shape
    assert batch == 1 and d == N_HEADS * HEAD_DIM
    assert ffn_w13.shape[0] == 2 and a_w_in.shape[0] == 1 and b_w_q.shape[0] == 1

    cos, sin_signed = _rope_tables(s)
    row = lambda g: g.reshape(1, -1)

    h = x[0]
    hb, inv = _norm_prep(h, row(ffn_norm[0, 0]))
    h, hb, inv = _ffn_half_step(h, hb, inv, ffn_w13, ffn_w2, 0, 0, [row(mix_norm[0])])
    z = _gelu_in(hb, inv, a_w_in[0])
    gated = _sgu_mix(z, row(a_sgu_norm[0]), a_w_spatial[0], a_b_spatial[0][:, :, None])
    h, hb, inv = _matmul_residual(gated, a_w_out[0], h, [row(ffn_norm[0, 1])],
                                  tm=OUT_PROJ_TM, tn=OUT_PROJ_TN)
    h, hb_kv, hb, inv = _ffn_half_step(h, hb, inv, ffn_w13, ffn_w2, 0, 1,
                                       [row(kv_norm), row(ffn_norm[1, 0])])

    kv = _rope_proj(hb_kv, inv, w_kv, cos, sin_signed, rope_cols=d, scale=1.0)

    h, hb, inv = _ffn_half_step(h, hb, inv, ffn_w13, ffn_w2, 1, 0, [row(mix_norm[1])])
    q = _rope_proj(hb, inv, b_w_q[0], cos, sin_signed, rope_cols=d,
                   scale=math.log2(math.e) / math.sqrt(HEAD_DIM))
    attn = _moba_attention(q, kv)
    h, hb, inv = _matmul_residual(attn, b_w_o[0], h, [row(ffn_norm[1, 1])],
                                  tm=OUT_PROJ_TM, tn=OUT_PROJ_TN)
    h, inv = _ffn_half_step(h, hb, inv, ffn_w13, ffn_w2, 1, 1, [])
    return _scale_rows(h, inv, row(final_norm))[None]
```

```python
import functools
import math

import jax
import jax.numpy as jnp
import numpy as np
from jax import lax
from jax.experimental import pallas as pl
from jax.experimental.pallas import tpu as pltpu

F32 = jnp.float32
BF16 = jnp.bfloat16

NORM_EPS = 1e-6
NEG_BIG = -1e30
MACARON_WEIGHT = 0.5
ROPE_THETA = 10000.0

SGU_CHUNK = 128
SGU_GROUPS = 32
N_HEADS = 32
HEAD_DIM = 128
MOBA_BLOCK = 256
MOBA_TOPK = 3
MOBA_TILE = 2048
MOBA_KEY_TILE = 2048

LANES = 128
VMEM_LIMIT_BYTES = 56 * 1024 * 1024

NORMED_TM = 2048
SWIGLU_TN = 256
PROJ_TN = 512
DOWN_TM, DOWN_TN = 512, 512
OUT_PROJ_TM, OUT_PROJ_TN = 1024, 512
ROWWISE_TM = 256
SGU_TM = 512


def _params(semantics):
    return pltpu.CompilerParams(dimension_semantics=semantics, vmem_limit_bytes=VMEM_LIMIT_BYTES)


def _row_tile_spec(tm, d):
    return pl.BlockSpec((tm, d), lambda i, j: (i, 0), pipeline_mode=pl.Buffered(1))


def _mxu_weight(w_ref):
    w = w_ref[...]
    return w if w.dtype == BF16 else w.astype(BF16)


def _norm_prep_kernel(x_ref, g_ref, xb_ref, inv_ref):
    x = x_ref[...]
    inv_ref[...] = lax.rsqrt(jnp.mean(x * x, axis=-1, keepdims=True) + NORM_EPS)
    xb_ref[...] = (x_ref[...] * g_ref[...]).astype(xb_ref.dtype)


def _norm_prep(x, gain, *, tm=ROWWISE_TM):
    s, d = x.shape
    return pl.pallas_call(
        _norm_prep_kernel,
        grid=(s // tm,),
        in_specs=[pl.BlockSpec((tm, d), lambda i: (i, 0)), pl.BlockSpec((1, d), lambda i: (0, 0))],
        out_specs=[pl.BlockSpec((tm, d), lambda i: (i, 0)), pl.BlockSpec((tm, 1), lambda i: (i, 0))],
        out_shape=[jax.ShapeDtypeStruct((s, d), BF16), jax.ShapeDtypeStruct((s, 1), F32)],
        compiler_params=_params(("parallel",)),
        name="norm_prep",
    )(x, gain)


def _swiglu_up_kernel(xb_ref, inv_ref, w13_hbm, w2_hbm, o_hbm, w2o_hbm, *, layer, half, tn, w2_rows):
    i = pl.program_id(0)
    tm, d = xb_ref.shape
    w13 = w13_hbm.at[layer, half]
    w2 = w2_hbm.at[layer, half]
    nj = w13.shape[1] // (2 * tn)

    def column_tile(wg_ref, wu_ref, w2_ref, o_ref, w2o_ref):
        w = jnp.concatenate([_mxu_weight(wg_ref), _mxu_weight(wu_ref)], axis=1)
        gu = jnp.dot(xb_ref[...], w, preferred_element_type=F32)
        inv = inv_ref[...]
        o_ref[...] = (jax.nn.silu(gu[:, :tn] * inv) * (gu[:, tn:] * (MACARON_WEIGHT * inv))).astype(o_ref.dtype)
        w2o_ref[...] = w2_ref[...].astype(w2o_ref.dtype)

    pltpu.emit_pipeline(
        column_tile,
        grid=(nj,),
        in_specs=[
            pl.BlockSpec((d, tn), lambda j: (0, j)),
            pl.BlockSpec((d, tn), lambda j: (0, j + nj)),
            pl.BlockSpec((w2_rows, d), lambda j: (i * nj + j, 0)),
        ],
        out_specs=[
            pl.BlockSpec((tm, tn), lambda j: (i, j)),
            pl.BlockSpec((w2_rows, d), lambda j: (i * nj + j, 0)),
        ],
    )(w13, w13, w2, o_hbm, w2o_hbm)


def _swiglu_up(xb, inv, w13, w2, layer, half, *, tm=NORMED_TM, tn=SWIGLU_TN):
    s, d = xb.shape
    f = w13.shape[-1] // 2
    nj = f // tn
    steps = (s // tm) * nj
    w2_rows = f // steps
    assert w2_rows * steps == f and w2_rows % 16 == 0
    in_hbm = pl.BlockSpec(memory_space=pl.ANY)
    return pl.pallas_call(
        functools.partial(_swiglu_up_kernel, layer=layer, half=half, tn=tn, w2_rows=w2_rows),
        grid=(s // tm,),
        in_specs=[
            pl.BlockSpec((tm, d), lambda i: (i, 0), pipeline_mode=pl.Buffered(1)),
            pl.BlockSpec((tm, 1), lambda i: (i, 0)),
            in_hbm,
            in_hbm,
        ],
        out_specs=[in_hbm, in_hbm],
        out_shape=[jax.ShapeDtypeStruct((s, f), BF16), jax.ShapeDtypeStruct((f, d), BF16)],
        compiler_params=_params(("arbitrary",)),
        name="swiglu_up",
    )(xb, inv, w13, w2)


def _gelu_in_kernel(xb_ref, inv_ref, w_ref, o_ref):
    z = jnp.dot(xb_ref[...], _mxu_weight(w_ref), preferred_element_type=F32) * inv_ref[...]
    o_ref[...] = (0.5 * z * (1.0 + lax.erf(z * math.sqrt(0.5)))).astype(o_ref.dtype)


def _gelu_in(xb, inv, w, *, tm=NORMED_TM, tn=PROJ_TN):
    s, d = xb.shape
    n = w.shape[-1]
    return pl.pallas_call(
        _gelu_in_kernel,
        grid=(s // tm, n // tn),
        in_specs=[
            _row_tile_spec(tm, d),
            pl.BlockSpec((tm, 1), lambda i, j: (i, 0)),
            pl.BlockSpec((d, tn), lambda i, j: (0, j)),
        ],
        out_specs=pl.BlockSpec((tm, tn), lambda i, j: (i, j)),
        out_shape=jax.ShapeDtypeStruct((s, n), BF16),
        compiler_params=_params(("parallel", "parallel")),
        name="gelu_in",
    )(xb, inv, w)


def _rope_store(acc, cos, sin_signed, o_ref):
    for h in range(acc.shape[1] // HEAD_DIM):
        cols = slice(h * HEAD_DIM, (h + 1) * HEAD_DIM)
        t = acc[:, cols]
        out = t * cos + pltpu.roll(t, HEAD_DIM // 2, axis=1) * sin_signed
        o_ref[:, cols] = out.astype(o_ref.dtype)


def _rope_proj_kernel(xb_ref, inv_ref, w_ref, cos_ref, sin_ref, o_ref, *, n_rope_tiles, scale):
    j = pl.program_id(1)
    row_scale = inv_ref[...] if scale == 1.0 else inv_ref[...] * scale

    @pl.when(j < n_rope_tiles)
    def _():
        acc = jnp.dot(xb_ref[...], _mxu_weight(w_ref), preferred_element_type=F32) * row_scale
        _rope_store(acc, cos_ref[...], sin_ref[...], o_ref)

    @pl.when(j >= n_rope_tiles)
    def _():
        o_ref[...] = (jnp.dot(xb_ref[...], _mxu_weight(w_ref), preferred_element_type=F32)
                      * inv_ref[...]).astype(o_ref.dtype)


def _rope_proj(xb, inv, w, cos, sin_signed, *, rope_cols, scale, tm=NORMED_TM, tn=PROJ_TN):
    s, d = xb.shape
    n = w.shape[-1]
    assert rope_cols == n or scale == 1.0
    kern = functools.partial(_rope_proj_kernel, n_rope_tiles=rope_cols // tn, scale=scale)
    return pl.pallas_call(
        kern,
        grid=(s // tm, n // tn),
        in_specs=[
            _row_tile_spec(tm, d),
            pl.BlockSpec((tm, 1), lambda i, j: (i, 0)),
            pl.BlockSpec((d, tn), lambda i, j: (0, j)),
            pl.BlockSpec((tm, HEAD_DIM), lambda i, j: (i, 0)),
            pl.BlockSpec((tm, HEAD_DIM), lambda i, j: (i, 0)),
        ],
        out_specs=pl.BlockSpec((tm, tn), lambda i, j: (i, j)),
        out_shape=jax.ShapeDtypeStruct((s, n), BF16),
        compiler_params=_params(("parallel", "parallel")),
        name="rope_proj",
    )(xb, inv, w, cos, sin_signed)


def _matmul_residual_kernel(a_ref, w_ref, r_ref, *rest, n_gains, d):
    gain_refs = rest[:n_gains]
    o_ref = rest[n_gains]
    xb_refs = rest[n_gains + 1:2 * n_gains + 1]
    inv_ref = rest[2 * n_gains + 1]
    j = pl.program_id(1)

    y = r_ref[...] + jnp.dot(a_ref[...], _mxu_weight(w_ref), preferred_element_type=F32)
    o_ref[...] = y
    for g_ref, xb_ref in zip(gain_refs, xb_refs):
        xb_ref[...] = (y * g_ref[...]).astype(xb_ref.dtype)

    ss = jnp.sum(y * y, axis=-1, keepdims=True)

    @pl.when(j == 0)
    def _():
        inv_ref[...] = ss

    @pl.when(j > 0)
    def _():
        inv_ref[...] += ss

    @pl.when(j == pl.num_programs(1) - 1)
    def _():
        inv_ref[...] = lax.rsqrt(inv_ref[...] * (1.0 / d) + NORM_EPS)


def _matmul_residual(a, w, res, gains, *, tm, tn):
    s, k = a.shape
    n = w.shape[1]
    n_gains = len(gains)
    tile = lambda dtype: (pl.BlockSpec((tm, tn), lambda i, j: (i, j)), jax.ShapeDtypeStruct((s, n), dtype))
    outs = [tile(F32)] + [tile(BF16)] * n_gains
    outs.append((pl.BlockSpec((tm, 1), lambda i, j: (i, 0)), jax.ShapeDtypeStruct((s, 1), F32)))
    return pl.pallas_call(
        functools.partial(_matmul_residual_kernel, n_gains=n_gains, d=n),
        grid=(s // tm, n // tn),
        in_specs=[
            pl.BlockSpec((tm, k), lambda i, j: (i, 0)),
            pl.BlockSpec((k, tn), lambda i, j: (0, j)),
            pl.BlockSpec((tm, tn), lambda i, j: (i, j)),
        ] + [pl.BlockSpec((1, tn), lambda i, j: (0, j))] * n_gains,
        out_specs=[spec for spec, _ in outs],
        out_shape=[shape for _, shape in outs],
        compiler_params=_params(("parallel", "arbitrary")),
        name="matmul_residual",
    )(a, w, res, *gains)


def _sgu_kernel(u_ref, v_ref, gn_ref, ws_ref, b_ref, o_ref, wsm_ref, vn_ref):
    @pl.when(pl.program_id(0) == 0)
    def _():
        t = lax.broadcasted_iota(jnp.int32, (SGU_CHUNK, SGU_CHUNK), 0)
        s = lax.broadcasted_iota(jnp.int32, (SGU_CHUNK, SGU_CHUNK), 1)
        for g in range(SGU_GROUPS):
            wsm_ref[g] = jnp.where(s <= t, ws_ref[g], 0.0).astype(wsm_ref.dtype)

    def chunk(c, carry):
        rows = pl.ds(pl.multiple_of(c * SGU_CHUNK, SGU_CHUNK), SGU_CHUNK)
        v = v_ref[rows, :].astype(F32)
        inv = lax.rsqrt(jnp.mean(v * v, axis=-1, keepdims=True) + NORM_EPS)
        vn_ref[...] = (v * inv * gn_ref[...]).astype(vn_ref.dtype)
        for g in range(SGU_GROUPS):
            cols = slice(g * LANES, (g + 1) * LANES)
            mixed = jnp.dot(wsm_ref[g], vn_ref[:, cols], preferred_element_type=F32) + b_ref[g]
            o_ref[rows, cols] = (u_ref[rows, cols].astype(F32) * mixed).astype(o_ref.dtype)
        return carry

    lax.fori_loop(0, u_ref.shape[0] // SGU_CHUNK, chunk, 0)


def _sgu_mix(z, sgu_gain, w_spatial, b_spatial, *, tm=SGU_TM):
    s, two_e = z.shape
    e = two_e // 2
    return pl.pallas_call(
        _sgu_kernel,
        grid=(s // tm,),
        in_specs=[
            pl.BlockSpec((tm, e), lambda c: (c, 0)),
            pl.BlockSpec((tm, e), lambda c: (c, 1)),
            pl.BlockSpec((1, e), lambda c: (0, 0)),
            pl.BlockSpec((SGU_GROUPS, SGU_CHUNK, SGU_CHUNK), lambda c: (0, 0, 0)),
            pl.BlockSpec((SGU_GROUPS, SGU_CHUNK, 1), lambda c: (0, 0, 0)),
        ],
        out_specs=pl.BlockSpec((tm, e), lambda c: (c, 0)),
        out_shape=jax.ShapeDtypeStruct((s, e), BF16),
        scratch_shapes=[
            pltpu.VMEM((SGU_GROUPS, SGU_CHUNK, SGU_CHUNK), BF16),
            pltpu.VMEM((SGU_CHUNK, e), BF16),
        ],
        compiler_params=_params(("arbitrary",)),
        name="sgu_mix",
    )(z, z, sgu_gain, w_spatial, b_spatial)


def _moba_kernel(q_ref, k_ref, v_ref, o_ref, kaug_ref, vaug_ref, qaug_ref, kmean_ref, m_ref, acc_ref):
    blk, tile, ktile = MOBA_BLOCK, MOBA_TILE, MOBA_KEY_TILE
    seq = k_ref.shape[0]
    n_blocks = seq // blk
    dn = (((1,), (1,)), ((), ()))

    lane = lax.broadcasted_iota(jnp.int32, (blk, LANES), 1)

    def key_prep(b, carry):
        rows = pl.ds(pl.multiple_of(b * blk, blk), blk)
        kb = k_ref[rows, :]
        kaug_ref[rows, 0:HEAD_DIM] = kb
        kaug_ref[rows, HEAD_DIM:2 * HEAD_DIM] = (lane == b).astype(BF16)
        vaug_ref[rows, 0:HEAD_DIM] = v_ref[rows, :]
        vaug_ref[rows, HEAD_DIM:2 * HEAD_DIM] = jnp.ones((blk, LANES), BF16)
        kmean_ref[pl.ds(b, 1), :] = jnp.mean(kb.astype(F32), axis=0, keepdims=True)
        return carry

    lax.fori_loop(0, n_blocks, key_prep, 0)
    kmean = kmean_ref[...]
    kmean_hi = kmean.astype(BF16)
    kmean_lo = (kmean - kmean_hi.astype(F32)).astype(BF16)

    def query_tile(qt, carry):
        rows = pl.ds(pl.multiple_of(qt * tile, tile), tile)
        q = q_ref[rows, :]

        gate = (lax.dot_general(kmean_hi, q, dn, preferred_element_type=F32)
                + lax.dot_general(kmean_lo, q, dn, preferred_element_type=F32))
        n_iota = lax.broadcasted_iota(jnp.int32, (n_blocks, tile), 0)
        q_blk = qt * (tile // blk) + lax.broadcasted_iota(jnp.int32, (n_blocks, tile), 1) // blk
        past = n_iota < q_blk
        g = jnp.where(past, gate, NEG_BIG)
        sel = jnp.zeros((n_blocks, tile), jnp.bool_)
        for _ in range(MOBA_TOPK):
            mx = jnp.max(g, axis=0, keepdims=True)
            first = jnp.min(jnp.where(g == mx, n_iota, n_blocks), axis=0, keepdims=True)
            pick = n_iota == first
            sel = jnp.logical_or(sel, pick)
            g = jnp.where(pick, -jnp.inf, g)
        allowed = jnp.logical_or(jnp.logical_and(sel, past), n_iota == q_blk)
        mask = jnp.where(allowed, 0.0, NEG_BIG)
        mask = jnp.concatenate([mask, jnp.zeros((LANES - n_blocks, tile), F32)], axis=0)
        qaug_ref[:, 0:HEAD_DIM] = q
        qaug_ref[:, HEAD_DIM:2 * HEAD_DIM] = mask.T.astype(BF16)

        q_groups = [slice(r * blk, (r + 1) * blk) for r in range(tile // blk)]

        row = lax.broadcasted_iota(jnp.int32, (blk, blk), 0)
        col = lax.broadcasted_iota(jnp.int32, (blk, blk), 1)
        seen = [pl.ds(pl.multiple_of(qt * tile, tile), (r + 1) * blk) for r in range(tile // blk)]
        scores = [lax.dot_general(qaug_ref[rr, :], kaug_ref[seen[r], :], dn,
                                  preferred_element_type=F32) for r, rr in enumerate(q_groups)]
        for r, rr in enumerate(q_groups):
            s = scores[r]
            s_own = jnp.where(col <= row, s[:, r * blk:], NEG_BIG)
            s = jnp.concatenate([s[:, :r * blk], s_own], axis=1) if r else s_own
            m0 = jnp.max(s, axis=1, keepdims=True)
            p = jnp.exp2(s - m0)
            m_ref[rr, :] = m0
            acc_ref[rr, :] = jnp.dot(p.astype(BF16), vaug_ref[seen[r], :],
                                     preferred_element_type=F32)

        def past_tile(t, c):
            krows = pl.ds(pl.multiple_of(t * ktile, ktile), ktile)
            scores = [lax.dot_general(qaug_ref[rr, :], kaug_ref[krows, :], dn,
                                      preferred_element_type=F32) for rr in q_groups]
            for rr, sn in zip(q_groups, scores):
                m_prev = m_ref[rr, :]
                m_new = jnp.maximum(m_prev, jnp.max(sn, axis=1, keepdims=True))
                pn = jnp.exp2(sn - m_new)
                acc_ref[rr, :] = jnp.exp2(m_prev - m_new) * acc_ref[rr, :] + jnp.dot(
                    pn.astype(BF16), vaug_ref[krows, :], preferred_element_type=F32)
                m_ref[rr, :] = m_new
            return c

        lax.fori_loop(0, qt * (tile // ktile), past_tile, 0)
        acc = acc_ref[...]
        o_ref[rows, :] = (acc[:, :HEAD_DIM] / acc[:, HEAD_DIM:HEAD_DIM + 1]).astype(o_ref.dtype)
        return carry

    lax.fori_loop(0, seq // tile, query_tile, 0)


def _moba_attention(q, kv):
    s = q.shape[0]
    assert s % MOBA_TILE == 0 and s // MOBA_BLOCK <= LANES
    head = lambda off: pl.BlockSpec((s, HEAD_DIM), lambda h: (0, h + off))
    return pl.pallas_call(
        _moba_kernel,
        grid=(N_HEADS,),
        in_specs=[head(0), head(0), head(N_HEADS)],
        out_specs=head(0),
        out_shape=jax.ShapeDtypeStruct(q.shape, BF16),
        scratch_shapes=[
            pltpu.VMEM((s, 2 * HEAD_DIM), BF16),
            pltpu.VMEM((s, 2 * HEAD_DIM), BF16),
            pltpu.VMEM((MOBA_TILE, 2 * HEAD_DIM), BF16),
            pltpu.VMEM((s // MOBA_BLOCK, HEAD_DIM), F32),
            pltpu.VMEM((MOBA_TILE, 1), F32),
            pltpu.VMEM((MOBA_TILE, 2 * HEAD_DIM), F32),
        ],
        compiler_params=_params(("parallel",)),
        name="moba_attention",
    )(q, kv, kv)


def _scale_rows_kernel(x_ref, inv_ref, g_ref, o_ref):
    o_ref[...] = x_ref[...] * inv_ref[...] * g_ref[...]


def _scale_rows(x, inv, gain, *, tm=ROWWISE_TM):
    s, d = x.shape
    return pl.pallas_call(
        _scale_rows_kernel,
        grid=(s // tm,),
        in_specs=[
            pl.BlockSpec((tm, d), lambda i: (i, 0)),
            pl.BlockSpec((tm, 1), lambda i: (i, 0)),
            pl.BlockSpec((1, d), lambda i: (0, 0)),
        ],
        out_specs=pl.BlockSpec((tm, d), lambda i: (i, 0)),
        out_shape=jax.ShapeDtypeStruct((s, d), F32),
        compiler_params=_params(("parallel",)),
        name="final_rmsnorm",
    )(x, inv, gain)


def _rope_tables(s):
    inv_freq = 1.0 / np.power(ROPE_THETA, np.arange(0, HEAD_DIM, 2, dtype=np.float64) / HEAD_DIM)
    ang = np.arange(s, dtype=np.float64)[:, None] * inv_freq[None, :]
    cos, sin = np.cos(ang), np.sin(ang)
    return (jnp.asarray(np.concatenate([cos, cos], axis=-1), dtype=F32),
            jnp.asarray(np.concatenate([-sin, sin], axis=-1), dtype=F32))


def _ffn_half_step(x, xb, inv, w13, w2, layer, half, next_gains):
    hmid, w2_bf16 = _swiglu_up(xb, inv, w13, w2, layer, half)
    return _matmul_residual(hmid, w2_bf16, x, next_gains, tm=DOWN_TM, tn=DOWN_TN)


def kernel(x, ffn_norm, ffn_w13, ffn_w2, mix_norm, a_w_in, a_sgu_norm, a_w_spatial, a_b_spatial,
           a_w_out, kv_norm, w_kv, b_w_q, b_w_o, final_norm):
    batch, s, d = x.shape
    assert batch == 1 and d == N_HEADS * HEAD_DIM
    assert ffn_w13.shape[0] == 2 and a_w_in.shape[0] == 1 and b_w_q.shape[0] == 1

    cos, sin_signed = _rope_tables(s)
    row = lambda g: g.reshape(1, -1)

    h = x[0]
    hb, inv = _norm_prep(h, row(ffn_norm[0, 0]))
    h, hb, inv = _ffn_half_step(h, hb, inv, ffn_w13, ffn_w2, 0, 0, [row(mix_norm[0])])
    z = _gelu_in(hb, inv, a_w_in[0])
    gated = _sgu_mix(z, row(a_sgu_norm[0]), a_w_spatial[0], a_b_spatial[0][:, :, None])
    h, hb, inv = _matmul_residual(gated, a_w_out[0], h, [row(ffn_norm[0, 1])],
                                  tm=OUT_PROJ_TM, tn=OUT_PROJ_TN)
    h, hb_kv, hb, inv = _ffn_half_step(h, hb, inv, ffn_w13, ffn_w2, 0, 1,
                                       [row(kv_norm), row(ffn_norm[1, 0])])

    kv = _rope_proj(hb_kv, inv, w_kv, cos, sin_signed, rope_cols=d, scale=1.0)

    h, hb, inv = _ffn_half_step(h, hb, inv, ffn_w13, ffn_w2, 1, 0, [row(mix_norm[1])])
    q = _rope_proj(hb, inv, b_w_q[0], cos, sin_signed, rope_cols=d,
                   scale=math.log2(math.e) / math.sqrt(HEAD_DIM))
    attn = _moba_attention(q, kv)
    h, hb, inv = _matmul_residual(attn, b_w_o[0], h, [row(ffn_norm[1, 1])],
                                  tm=OUT_PROJ_TM, tn=OUT_PROJ_TN)
    h, inv = _ffn_half_step(h, hb, inv, ffn_w13, ffn_w2, 1, 1, [])
    return _scale_rows(h, inv, row(final_norm))[None]
```

```python
import functools
import math

import jax
import jax.numpy as jnp
import numpy as np
from jax import lax
from jax.experimental import pallas as pl
from jax.experimental.pallas import tpu as pltpu

F32 = jnp.float32
BF16 = jnp.bfloat16

NORM_EPS = 1e-6
NEG_BIG = -1e30
MACARON_WEIGHT = 0.5
ROPE_THETA = 10000.0

SGU_CHUNK = 128
SGU_GROUPS = 32
N_HEADS = 32
HEAD_DIM = 128
MOBA_BLOCK = 256
MOBA_TOPK = 3
MOBA_TILE = 2048
MOBA_KEY_TILE = 2048

LANES = 128
VMEM_LIMIT_BYTES = 56 * 1024 * 1024

NORMED_TM = 2048
SWIGLU_TN = 256
PROJ_TN = 512
DOWN_TM, DOWN_TN = 512, 512
OUT_PROJ_TM, OUT_PROJ_TN = 1024, 512
ROWWISE_TM = 256
SGU_TM = 512


def _params(semantics):
    return pltpu.CompilerParams(dimension_semantics=semantics, vmem_limit_bytes=VMEM_LIMIT_BYTES)


def _row_tile_spec(tm, d):
    return pl.BlockSpec((tm, d), lambda i, j: (i, 0), pipeline_mode=pl.Buffered(1))


def _mxu_weight(w_ref):
    w = w_ref[...]
    return w if w.dtype == BF16 else w.astype(BF16)


def _norm_prep_kernel(x_ref, g_ref, xb_ref, inv_ref):
    x = x_ref[...]
    inv_ref[...] = lax.rsqrt(jnp.mean(x * x, axis=-1, keepdims=True) + NORM_EPS)
    xb_ref[...] = (x_ref[...] * g_ref[...]).astype(xb_ref.dtype)


def _norm_prep(x, gain, *, tm=ROWWISE_TM):
    s, d = x.shape
    return pl.pallas_call(
        _norm_prep_kernel,
        grid=(s // tm,),
        in_specs=[pl.BlockSpec((tm, d), lambda i: (i, 0)), pl.BlockSpec((1, d), lambda i: (0, 0))],
        out_specs=[pl.BlockSpec((tm, d), lambda i: (i, 0)), pl.BlockSpec((tm, 1), lambda i: (i, 0))],
        out_shape=[jax.ShapeDtypeStruct((s, d), BF16), jax.ShapeDtypeStruct((s, 1), F32)],
        compiler_params=_params(("parallel",)),
        name="norm_prep",
    )(x, gain)


def _swiglu_up_kernel(xb_ref, inv_ref, wg_ref, wu_ref, w2_ref, o_ref, w2o_ref):
    tn = wg_ref.shape[1]
    w = jnp.concatenate([_mxu_weight(wg_ref), _mxu_weight(wu_ref)], axis=1)
    gu = jnp.dot(xb_ref[...], w, preferred_element_type=F32)
    inv = inv_ref[...]
    o_ref[...] = (jax.nn.silu(gu[:, :tn] * inv) * (gu[:, tn:] * (MACARON_WEIGHT * inv))).astype(o_ref.dtype)
    w2o_ref[...] = w2_ref[...].astype(w2o_ref.dtype)


def _swiglu_up(xb, inv, w13, w2, layer, half, *, tm=NORMED_TM, tn=SWIGLU_TN):
    s, d = xb.shape
    f = w13.shape[-1] // 2
    nj = f // tn
    steps = (s // tm) * nj
    w2_rows = f // steps
    assert w2_rows * steps == f and w2_rows % 16 == 0
    return pl.pallas_call(
        _swiglu_up_kernel,
        grid=(s // tm, nj),
        in_specs=[
            _row_tile_spec(tm, d),
            pl.BlockSpec((tm, 1), lambda i, j: (i, 0)),
            pl.BlockSpec((None, None, d, tn), lambda i, j: (layer, half, 0, j)),
            pl.BlockSpec((None, None, d, tn), lambda i, j: (layer, half, 0, j + nj)),
            pl.BlockSpec((None, None, w2_rows, d), lambda i, j: (layer, half, i * nj + j, 0)),
        ],
        out_specs=[
            pl.BlockSpec((tm, tn), lambda i, j: (i, j)),
            pl.BlockSpec((w2_rows, d), lambda i, j: (i * nj + j, 0)),
        ],
        out_shape=[jax.ShapeDtypeStruct((s, f), BF16), jax.ShapeDtypeStruct((f, d), BF16)],
        compiler_params=_params(("parallel", "parallel")),
        name="swiglu_up",
    )(xb, inv, w13, w13, w2)


def _gelu_in_kernel(xb_ref, inv_ref, w_ref, o_ref):
    z = jnp.dot(xb_ref[...], _mxu_weight(w_ref), preferred_element_type=F32) * inv_ref[...]
    o_ref[...] = (0.5 * z * (1.0 + lax.erf(z * math.sqrt(0.5)))).astype(o_ref.dtype)


def _gelu_in(xb, inv, w, *, tm=NORMED_TM, tn=PROJ_TN):
    s, d = xb.shape
    n = w.shape[-1]
    return pl.pallas_call(
        _gelu_in_kernel,
        grid=(s // tm, n // tn),
        in_specs=[
            _row_tile_spec(tm, d),
            pl.BlockSpec((tm, 1), lambda i, j: (i, 0)),
            pl.BlockSpec((d, tn), lambda i, j: (0, j)),
        ],
        out_specs=pl.BlockSpec((tm, tn), lambda i, j: (i, j)),
        out_shape=jax.ShapeDtypeStruct((s, n), BF16),
        compiler_params=_params(("parallel", "parallel")),
        name="gelu_in",
    )(xb, inv, w)


def _rope_store(acc, cos, sin_signed, o_ref):
    for h in range(acc.shape[1] // HEAD_DIM):
        cols = slice(h * HEAD_DIM, (h + 1) * HEAD_DIM)
        t = acc[:, cols]
        out = t * cos + pltpu.roll(t, HEAD_DIM // 2, axis=1) * sin_signed
        o_ref[:, cols] = out.astype(o_ref.dtype)


def _rope_proj_kernel(xb_ref, inv_ref, w_ref, cos_ref, sin_ref, o_ref, *, n_rope_tiles, scale):
    j = pl.program_id(1)
    row_scale = inv_ref[...] if scale == 1.0 else inv_ref[...] * scale

    @pl.when(j < n_rope_tiles)
    def _():
        acc = jnp.dot(xb_ref[...], _mxu_weight(w_ref), preferred_element_type=F32) * row_scale
        _rope_store(acc, cos_ref[...], sin_ref[...], o_ref)

    @pl.when(j >= n_rope_tiles)
    def _():
        o_ref[...] = (jnp.dot(xb_ref[...], _mxu_weight(w_ref), preferred_element_type=F32)
                      * inv_ref[...]).astype(o_ref.dtype)


def _rope_proj(xb, inv, w, cos, sin_signed, *, rope_cols, scale, tm=NORMED_TM, tn=PROJ_TN):
    s, d = xb.shape
    n = w.shape[-1]
    assert rope_cols == n or scale == 1.0
    kern = functools.partial(_rope_proj_kernel, n_rope_tiles=rope_cols // tn, scale=scale)
    return pl.pallas_call(
        kern,
        grid=(s // tm, n // tn),
        in_specs=[
            _row_tile_spec(tm, d),
            pl.BlockSpec((tm, 1), lambda i, j: (i, 0)),
            pl.BlockSpec((d, tn), lambda i, j: (0, j)),
            pl.BlockSpec((tm, HEAD_DIM), lambda i, j: (i, 0)),
            pl.BlockSpec((tm, HEAD_DIM), lambda i, j: (i, 0)),
        ],
        out_specs=pl.BlockSpec((tm, tn), lambda i, j: (i, j)),
        out_shape=jax.ShapeDtypeStruct((s, n), BF16),
        compiler_params=_params(("parallel", "parallel")),
        name="rope_proj",
    )(xb, inv, w, cos, sin_signed)


def _matmul_residual_kernel(a_ref, w_ref, r_ref, *rest, n_gains, d):
    gain_refs = rest[:n_gains]
    o_ref = rest[n_gains]
    xb_refs = rest[n_gains + 1:2 * n_gains + 1]
    inv_ref = rest[2 * n_gains + 1]
    j = pl.program_id(1)

    y = r_ref[...] + jnp.dot(a_ref[...], _mxu_weight(w_ref), preferred_element_type=F32)
    o_ref[...] = y
    for g_ref, xb_ref in zip(gain_refs, xb_refs):
        xb_ref[...] = (y * g_ref[...]).astype(xb_ref.dtype)

    ss = jnp.sum(y * y, axis=-1, keepdims=True)

    @pl.when(j == 0)
    def _():
        inv_ref[...] = ss

    @pl.when(j > 0)
    def _():
        inv_ref[...] += ss

    @pl.when(j == pl.num_programs(1) - 1)
    def _():
        inv_ref[...] = lax.rsqrt(inv_ref[...] * (1.0 / d) + NORM_EPS)


def _matmul_residual(a, w, res, gains, *, tm, tn):
    s, k = a.shape
    n = w.shape[1]
    n_gains = len(gains)
    tile = lambda dtype: (pl.BlockSpec((tm, tn), lambda i, j: (i, j)), jax.ShapeDtypeStruct((s, n), dtype))
    outs = [tile(F32)] + [tile(BF16)] * n_gains
    outs.append((pl.BlockSpec((tm, 1), lambda i, j: (i, 0)), jax.ShapeDtypeStruct((s, 1), F32)))
    return pl.pallas_call(
        functools.partial(_matmul_residual_kernel, n_gains=n_gains, d=n),
        grid=(s // tm, n // tn),
        in_specs=[
            pl.BlockSpec((tm, k), lambda i, j: (i, 0)),
            pl.BlockSpec((k, tn), lambda i, j: (0, j)),
            pl.BlockSpec((tm, tn), lambda i, j: (i, j)),
        ] + [pl.BlockSpec((1, tn), lambda i, j: (0, j))] * n_gains,
        out_specs=[spec for spec, _ in outs],
        out_shape=[shape for _, shape in outs],
        compiler_params=_params(("parallel", "arbitrary")),
        name="matmul_residual",
    )(a, w, res, *gains)


def _sgu_kernel(u_ref, v_ref, gn_ref, ws_ref, b_ref, o_ref, wsm_ref, vn_ref):
    @pl.when(pl.program_id(0) == 0)
    def _():
        t = lax.broadcasted_iota(jnp.int32, (SGU_CHUNK, SGU_CHUNK), 0)
        s = lax.broadcasted_iota(jnp.int32, (SGU_CHUNK, SGU_CHUNK), 1)
        for g in range(SGU_GROUPS):
            wsm_ref[g] = jnp.where(s <= t, ws_ref[g], 0.0).astype(wsm_ref.dtype)

    def chunk(c, carry):
        rows = pl.ds(pl.multiple_of(c * SGU_CHUNK, SGU_CHUNK), SGU_CHUNK)
        v = v_ref[rows, :].astype(F32)
        inv = lax.rsqrt(jnp.mean(v * v, axis=-1, keepdims=True) + NORM_EPS)
        vn_ref[...] = (v * inv * gn_ref[...]).astype(vn_ref.dtype)
        for g in range(SGU_GROUPS):
            cols = slice(g * LANES, (g + 1) * LANES)
            mixed = jnp.dot(wsm_ref[g], vn_ref[:, cols], preferred_element_type=F32) + b_ref[g]
            o_ref[rows, cols] = (u_ref[rows, cols].astype(F32) * mixed).astype(o_ref.dtype)
        return carry

    lax.fori_loop(0, u_ref.shape[0] // SGU_CHUNK, chunk, 0)


def _sgu_mix(z, sgu_gain, w_spatial, b_spatial, *, tm=SGU_TM):
    s, two_e = z.shape
    e = two_e // 2
    return pl.pallas_call(
        _sgu_kernel,
        grid=(s // tm,),
        in_specs=[
            pl.BlockSpec((tm, e), lambda c: (c, 0)),
            pl.BlockSpec((tm, e), lambda c: (c, 1)),
            pl.BlockSpec((1, e), lambda c: (0, 0)),
            pl.BlockSpec((SGU_GROUPS, SGU_CHUNK, SGU_CHUNK), lambda c: (0, 0, 0)),
            pl.BlockSpec((SGU_GROUPS, SGU_CHUNK, 1), lambda c: (0, 0, 0)),
        ],
        out_specs=pl.BlockSpec((tm, e), lambda c: (c, 0)),
        out_shape=jax.ShapeDtypeStruct((s, e), BF16),
        scratch_shapes=[
            pltpu.VMEM((SGU_GROUPS, SGU_CHUNK, SGU_CHUNK), BF16),
            pltpu.VMEM((SGU_CHUNK, e), BF16),
        ],
        compiler_params=_params(("arbitrary",)),
        name="sgu_mix",
    )(z, z, sgu_gain, w_spatial, b_spatial)


def _moba_kernel(q_ref, k_ref, v_ref, o_ref, kaug_ref, vaug_ref, qaug_ref, m_ref, acc_ref):
    blk, tile, ktile = MOBA_BLOCK, MOBA_TILE, MOBA_KEY_TILE
    seq = k_ref.shape[0]
    n_blocks = seq // blk
    dn = (((1,), (1,)), ((), ()))

    lane = lax.broadcasted_iota(jnp.int32, (blk, LANES), 1)

    @pl.when(pl.program_id(0) == 0)
    def _():
        def head_invariant(b, carry):
            rows = pl.ds(pl.multiple_of(b * blk, blk), blk)
            kaug_ref[rows, HEAD_DIM:2 * HEAD_DIM] = (lane == b).astype(BF16)
            vaug_ref[rows, HEAD_DIM:2 * HEAD_DIM] = jnp.ones((blk, LANES), BF16)
            return carry

        lax.fori_loop(0, n_blocks, head_invariant, 0)

    def key_prep(b, carry):
        rows = pl.ds(pl.multiple_of(b * blk, blk), blk)
        kaug_ref[rows, 0:HEAD_DIM] = k_ref[rows, :]
        vaug_ref[rows, 0:HEAD_DIM] = v_ref[rows, :]
        return carry

    lax.fori_loop(0, n_blocks, key_prep, 0)
    ksum = lax.dot_general(kaug_ref[:, HEAD_DIM:2 * HEAD_DIM], k_ref[...], (((0,), (0,)), ((), ())),
                           preferred_element_type=F32)
    kmean = ksum[:n_blocks, :] * (1.0 / blk)
    kmean_hi = kmean.astype(BF16)
    kmean_lo = (kmean - kmean_hi.astype(F32)).astype(BF16)

    def query_tile(qt, carry):
        rows = pl.ds(pl.multiple_of(qt * tile, tile), tile)
        q = q_ref[rows, :]

        gate = (lax.dot_general(kmean_hi, q, dn, preferred_element_type=F32)
                + lax.dot_general(kmean_lo, q, dn, preferred_element_type=F32))
        n_iota = lax.broadcasted_iota(jnp.int32, (n_blocks, tile), 0)
        q_blk = qt * (tile // blk) + lax.broadcasted_iota(jnp.int32, (n_blocks, tile), 1) // blk
        past = n_iota < q_blk
        g = jnp.where(past, gate, NEG_BIG)
        sel = jnp.zeros((n_blocks, tile), jnp.bool_)
        for _ in range(MOBA_TOPK):
            mx = jnp.max(g, axis=0, keepdims=True)
            first = jnp.min(jnp.where(g == mx, n_iota, n_blocks), axis=0, keepdims=True)
            pick = n_iota == first
            sel = jnp.logical_or(sel, pick)
            g = jnp.where(pick, -jnp.inf, g)
        allowed = jnp.logical_or(jnp.logical_and(sel, past), n_iota == q_blk)
        mask = jnp.where(allowed, 0.0, NEG_BIG)
        mask = jnp.concatenate([mask, jnp.zeros((LANES - n_blocks, tile), F32)], axis=0)
        qaug_ref[:, 0:HEAD_DIM] = q
        qaug_ref[:, HEAD_DIM:2 * HEAD_DIM] = mask.T.astype(BF16)

        q_groups = [slice(r * blk, (r + 1) * blk) for r in range(tile // blk)]

        row = lax.broadcasted_iota(jnp.int32, (blk, blk), 0)
        col = lax.broadcasted_iota(jnp.int32, (blk, blk), 1)
        seen = [pl.ds(pl.multiple_of(qt * tile, tile), (r + 1) * blk) for r in range(tile // blk)]
        scores = [lax.dot_general(qaug_ref[rr, :], kaug_ref[seen[r], :], dn,
                                  preferred_element_type=F32) for r, rr in enumerate(q_groups)]
        for r, rr in enumerate(q_groups):
            s = scores[r]
            s_own = jnp.where(col <= row, s[:, r * blk:], NEG_BIG)
            s = jnp.concatenate([s[:, :r * blk], s_own], axis=1) if r else s_own
            m0 = jnp.max(s, axis=1, keepdims=True)
            p = jnp.exp2(s - m0)
            m_ref[rr, :] = m0
            acc_ref[rr, :] = jnp.dot(p.astype(BF16), vaug_ref[seen[r], :],
                                     preferred_element_type=F32)

        def past_tile(t, c):
            krows = pl.ds(pl.multiple_of(t * ktile, ktile), ktile)
            scores = [lax.dot_general(qaug_ref[rr, :], kaug_ref[krows, :], dn,
                                      preferred_element_type=F32) for rr in q_groups]
            for rr, sn in zip(q_groups, scores):
                m_prev = m_ref[rr, :]
                m_new = jnp.maximum(m_prev, jnp.max(sn, axis=1, keepdims=True))
                pn = jnp.exp2(sn - m_new)
                acc_ref[rr, :] = jnp.exp2(m_prev - m_new) * acc_ref[rr, :] + jnp.dot(
                    pn.astype(BF16), vaug_ref[krows, :], preferred_element_type=F32)
                m_ref[rr, :] = m_new
            return c

        lax.fori_loop(0, qt * (tile // ktile), past_tile, 0)
        acc = acc_ref[...]
        o_ref[rows, :] = (acc[:, :HEAD_DIM] / acc[:, HEAD_DIM:HEAD_DIM + 1]).astype(o_ref.dtype)
        return carry

    lax.fori_loop(0, seq // tile, query_tile, 0)


def _moba_attention(q, kv):
    s = q.shape[0]
    assert s % MOBA_TILE == 0 and s // MOBA_BLOCK <= LANES
    head = lambda off: pl.BlockSpec((s, HEAD_DIM), lambda h: (0, h + off))
    return pl.pallas_call(
        _moba_kernel,
        grid=(N_HEADS,),
        in_specs=[head(0), head(0), head(N_HEADS)],
        out_specs=head(0),
        out_shape=jax.ShapeDtypeStruct(q.shape, BF16),
        scratch_shapes=[
            pltpu.VMEM((s, 2 * HEAD_DIM), BF16),
            pltpu.VMEM((s, 2 * HEAD_DIM), BF16),
            pltpu.VMEM((MOBA_TILE, 2 * HEAD_DIM), BF16),
            pltpu.VMEM((MOBA_TILE, 1), F32),
            pltpu.VMEM((MOBA_TILE, 2 * HEAD_DIM), F32),
        ],
        compiler_params=_params(("arbitrary",)),
        name="moba_attention",
    )(q, kv, kv)


def _scale_rows_kernel(x_ref, inv_ref, g_ref, o_ref):
    o_ref[...] = x_ref[...] * inv_ref[...] * g_ref[...]


def _scale_rows(x, inv, gain, *, tm=ROWWISE_TM):
    s, d = x.shape
    return pl.pallas_call(
        _scale_rows_kernel,
        grid=(s // tm,),
        in_specs=[
            pl.BlockSpec((tm, d), lambda i: (i, 0)),
            pl.BlockSpec((tm, 1), lambda i: (i, 0)),
            pl.BlockSpec((1, d), lambda i: (0, 0)),
        ],
        out_specs=pl.BlockSpec((tm, d), lambda i: (i, 0)),
        out_shape=jax.ShapeDtypeStruct((s, d), F32),
        compiler_params=_params(("parallel",)),
        name="final_rmsnorm",
    )(x, inv, gain)


def _rope_tables(s):
    inv_freq = 1.0 / np.power(ROPE_THETA, np.arange(0, HEAD_DIM, 2, dtype=np.float64) / HEAD_DIM)
    ang = np.arange(s, dtype=np.float64)[:, None] * inv_freq[None, :]
    cos, sin = np.cos(ang), np.sin(ang)
    return (jnp.asarray(np.concatenate([cos, cos], axis=-1), dtype=F32),
            jnp.asarray(np.concatenate([-sin, sin], axis=-1), dtype=F32))


def _ffn_half_step(x, xb, inv, w13, w2, layer, half, next_gains):
    hmid, w2_bf16 = _swiglu_up(xb, inv, w13, w2, layer, half)
    return _matmul_residual(hmid, w2_bf16, x, next_gains, tm=DOWN_TM, tn=DOWN_TN)


def kernel(x, ffn_norm, ffn_w13, ffn_w2, mix_norm, a_w_in, a_sgu_norm, a_w_spatial, a_b_spatial,
           a_w_out, kv_norm, w_kv, b_w_q, b_w_o, final_norm):
    batch, s, d = x.shape
    assert batch == 1 and d == N_HEADS * HEAD_DIM
    assert ffn_w13.shape[0] == 2 and a_w_in.shape[0] == 1 and b_w_q.shape[0] == 1

    cos, sin_signed = _rope_tables(s)
    row = lambda g: g.reshape(1, -1)

    h = x[0]
    hb, inv = _norm_prep(h, row(ffn_norm[0, 0]))
    h, hb, inv = _ffn_half_step(h, hb, inv, ffn_w13, ffn_w2, 0, 0, [row(mix_norm[0])])
    z = _gelu_in(hb, inv, a_w_in[0])
    gated = _sgu_mix(z, row(a_sgu_norm[0]), a_w_spatial[0], a_b_spatial[0][:, :, None])
    h, hb, inv = _matmul_residual(gated, a_w_out[0], h, [row(ffn_norm[0, 1])],
                                  tm=OUT_PROJ_TM, tn=OUT_PROJ_TN)
    h, hb_kv, hb, inv = _ffn_half_step(h, hb, inv, ffn_w13, ffn_w2, 0, 1,
                                       [row(kv_norm), row(ffn_norm[1, 0])])

    kv = _rope_proj(hb_kv, inv, w_kv, cos, sin_signed, rope_cols=d, scale=1.0)

    h, hb, inv = _ffn_half_step(h, hb, inv, ffn_w13, ffn_w2, 1, 0, [row(mix_norm[1])])
    q = _rope_proj(hb, inv, b_w_q[0], cos, sin_signed, rope_cols=d,
                   scale=math.log2(math.e) / math.sqrt(HEAD_DIM))
    attn = _moba_attention(q, kv)
    h, hb, inv = _matmul_residual(attn, b_w_o[0], h, [row(ffn_norm[1, 1])],
                                  tm=OUT_PROJ_TM, tn=OUT_PROJ_TN)
    h, inv = _ffn_half_step(h, hb, inv, ffn_w13, ffn_w2, 1, 1, [])
    return _scale_rows(h, inv, row(final_norm))[None]
```
